```python
import jax, jax.numpy as jnp
from jax import lax
import numpy as np

D_MODEL = 1024
BATCH = 2
SEQ = 16384
DEPTH = 4

D_MIX = D_MODEL
MLA_HEADS = 8
NOPE_DIM = 64
ROPE_DIM = 32
V_DIM = 64
QK_DIM = NOPE_DIM + ROPE_DIM
Q_RANK = 256
KV_RANK = 128
MLA_WIDTH = MLA_HEADS * V_DIM
LRU_WIDTH = D_MIX - MLA_WIDTH
LRU_BLOCKS = 8
LRU_BLOCK = LRU_WIDTH // LRU_BLOCKS
CONV_WIDTH = 4
LRU_C = 8.0
IN_SPLITS = (Q_RANK, Q_RANK + KV_RANK, Q_RANK + KV_RANK + ROPE_DIM,
             Q_RANK + KV_RANK + ROPE_DIM + LRU_WIDTH)
IN_COLS = Q_RANK + KV_RANK + ROPE_DIM + 2 * LRU_WIDTH
D_FF = 2816
N_EXPERTS = 8
TOP_K = 2
D_FF_EXPERT = 3584
N_DENSE = (DEPTH + 1) // 2
N_MOE = DEPTH // 2
ROPE_THETA = 10000.0
Q_BLOCK = 128
EPS = 1e-6

kernel_name = "hymba_mla_rglru_moe_adaln_trunk"


def _rms(x, gain):
    xf = x.astype(jnp.float32)
    y = xf * lax.rsqrt(jnp.mean(xf * xf, axis=-1, keepdims=True) + EPS)
    return (y * gain.astype(jnp.float32)).astype(x.dtype)


def _rope(x, cos, sin):
    x1, x2 = jnp.split(x, 2, axis=-1)
    return jnp.concatenate([x1 * cos - x2 * sin, x1 * sin + x2 * cos], axis=-1)


def _modulate(c_act, w, b):
    m = (c_act @ w + b)[:, None, :]
    shift, scale, gate = jnp.split(m, 3, axis=-1)
    return shift, scale, gate


def _causal_attention(q, k, v, positions):
    B, S, H, _ = q.shape
    nb = S // Q_BLOCK
    qb = q.reshape(B, nb, Q_BLOCK, H, QK_DIM).transpose(1, 0, 2, 3, 4)
    pb = positions.reshape(B, nb, Q_BLOCK).transpose(1, 0, 2)
    scale = QK_DIM ** -0.5
    neg = jnp.finfo(jnp.float32).min

    def one_block(args):
        q_blk, p_blk = args
        s = jnp.einsum('bqhd,bkhd->bhqk', q_blk, k,
                       preferred_element_type=jnp.float32) * scale
        mask = positions[:, None, None, :] <= p_blk[:, None, :, None]
        p = jax.nn.softmax(jnp.where(mask, s, neg), axis=-1).astype(v.dtype)
        return jnp.einsum('bhqk,bkhd->bqhd', p, v)

    o = lax.map(one_block, (qb, pb))
    return o.transpose(1, 0, 2, 3, 4).reshape(B, S, H * V_DIM)


def _causal_conv(x, w, b):
    y = lax.conv_general_dilated(
        x, w[:, None, :].astype(x.dtype), window_strides=(1,),
        padding=[(CONV_WIDTH - 1, 0)], dimension_numbers=('NWC', 'WIO', 'NWC'),
        feature_group_count=x.shape[-1])
    return y + b


def _rglru(x, w_a, b_a, w_x, b_x, lam):
    B, S, _ = x.shape
    xb = x.reshape(B, S, LRU_BLOCKS, LRU_BLOCK)
    r = jax.nn.sigmoid((jnp.einsum('bsgi,gij->bsgj', xb, w_a).reshape(B, S, LRU_WIDTH) + b_a).astype(jnp.float32))
    i = jax.nn.sigmoid((jnp.einsum('bsgi,gij->bsgj', xb, w_x).reshape(B, S, LRU_WIDTH) + b_x).astype(jnp.float32))
    log_a = LRU_C * r * jax.nn.log_sigmoid(lam.astype(jnp.float32))
    a = jnp.exp(log_a)
    u = jnp.sqrt(-jnp.expm1(2.0 * log_a)) * (i * x.astype(jnp.float32))

    def combine(left, right):
        a1, b1 = left
        a2, b2 = right
        return a1 * a2, a2 * b1 + b2

    _, h = lax.associative_scan(combine, (a, u), axis=1)
    return h.astype(x.dtype)


def _hybrid_mixer(h, positions, cos, sin, w_in, latent_gain, w_q_up, w_kv_up, q_gain, k_gain,
                  conv_w, conv_b, w_a, b_a, w_x, b_x, lam, out_gain, w_out):
    B, S, _ = h.shape
    z = h @ w_in
    c_q, c_kv, k_r, x_rec, g_rec = jnp.split(z, IN_SPLITS, axis=-1)
    c_q = _rms(c_q, latent_gain[:Q_RANK])
    c_kv = _rms(c_kv, latent_gain[Q_RANK:])
    q = (c_q @ w_q_up).reshape(B, S, MLA_HEADS, QK_DIM)
    kv = (c_kv @ w_kv_up).reshape(B, S, MLA_HEADS, NOPE_DIM + V_DIM)
    k_nope, v = kv[..., :NOPE_DIM], kv[..., NOPE_DIM:]
    q_nope = _rms(q[..., :NOPE_DIM], q_gain[:NOPE_DIM])
    q_rope = _rope(_rms(q[..., NOPE_DIM:], q_gain[NOPE_DIM:]), cos[:, :, None, :], sin[:, :, None, :])
    k_nope = _rms(k_nope, k_gain[:NOPE_DIM])
    k_rope = _rope(_rms(k_r, k_gain[NOPE_DIM:]), cos, sin)
    qf = jnp.concatenate([q_nope, q_rope], axis=-1)
    kf = jnp.concatenate([k_nope, jnp.broadcast_to(k_rope[:, :, None, :], (B, S, MLA_HEADS, ROPE_DIM))], axis=-1)
    attn = _causal_attention(qf, kf, v, positions)
    xr = _causal_conv(x_rec, conv_w, conv_b)
    rec = _rglru(xr, w_a, b_a, w_x, b_x, lam) * jax.nn.gelu(g_rec)
    y = jnp.concatenate([_rms(attn, out_gain[:MLA_WIDTH]), _rms(rec, out_gain[MLA_WIDTH:])], axis=-1)
    return y @ w_out


def _swiglu(h, wg, wu, wd):
    return (jax.nn.silu(h @ wg) * (h @ wu)) @ wd


def _moe(h, w_router, wg, wu, wd):
    B, S, D = h.shape
    t = h.reshape(B * S, D)
    logits = (t @ w_router).astype(jnp.float32)
    top_val, top_idx = lax.top_k(logits, TOP_K)
    top_w = jax.nn.softmax(top_val, axis=-1)
    combine = jnp.sum(jax.nn.one_hot(top_idx, N_EXPERTS, dtype=jnp.float32) * top_w[..., None], axis=1)
    combine = combine.astype(t.dtype)
    out = jnp.zeros_like(t)
    for e in range(N_EXPERTS):
        out = out + combine[:, e:e + 1] * _swiglu(t, wg[e], wu[e], wd[e])
    return out.reshape(B, S, D)


def setup_inputs(seed: int = 0) -> dict:
    key = jax.random.key(seed)
    ks = jax.random.split(key, 32)
    f32 = jnp.float32

    def nrm(k, shape, fan_in):
        return jax.random.normal(k, shape, f32) * (fan_in ** -0.5)

    def gain(k, shape):
        return 1.0 + 0.02 * jax.random.normal(k, shape, f32)

    x = jax.random.normal(ks[0], (BATCH, SEQ, D_MODEL), f32)
    c = jax.random.normal(ks[1], (BATCH, D_MODEL), f32)
    offset = jax.random.randint(ks[2], (BATCH, 1), 0, 1024, dtype=jnp.int32)
    positions = offset + jnp.arange(SEQ, dtype=jnp.int32)[None, :]

    w_ada = jax.random.normal(ks[3], (DEPTH, 2, D_MODEL, 3 * D_MODEL), f32) * (0.1 * D_MODEL ** -0.5)
    gate_offset = jnp.concatenate([jnp.zeros((2 * D_MODEL,), f32), jnp.ones((D_MODEL,), f32)])
    b_ada = 0.02 * jax.random.normal(ks[4], (DEPTH, 2, 3 * D_MODEL), f32) + gate_offset
    norm_gain = gain(ks[5], (DEPTH, 2, D_MODEL))

    w_in = nrm(ks[6], (DEPTH, D_MODEL, IN_COLS), D_MODEL)
    latent_gain = gain(ks[7], (DEPTH, Q_RANK + KV_RANK))
    w_q_up = nrm(ks[8], (DEPTH, Q_RANK, MLA_HEADS * QK_DIM), Q_RANK)
    w_kv_up = nrm(ks[9], (DEPTH, KV_RANK, MLA_HEADS * (NOPE_DIM + V_DIM)), KV_RANK)
    q_gain = gain(ks[10], (DEPTH, QK_DIM))
    k_gain = gain(ks[11], (DEPTH, QK_DIM))

    conv_w = nrm(ks[12], (DEPTH, CONV_WIDTH, LRU_WIDTH), CONV_WIDTH)
    conv_b = 0.01 * jax.random.normal(ks[13], (DEPTH, LRU_WIDTH), f32)
    w_a = nrm(ks[14], (DEPTH, LRU_BLOCKS, LRU_BLOCK, LRU_BLOCK), LRU_BLOCK)
    b_a = 0.01 * jax.random.normal(ks[15], (DEPTH, LRU_WIDTH), f32)
    w_x = nrm(ks[16], (DEPTH, LRU_BLOCKS, LRU_BLOCK, LRU_BLOCK), LRU_BLOCK)
    b_x = 0.01 * jax.random.normal(ks[17], (DEPTH, LRU_WIDTH), f32)
    a8 = jax.random.uniform(ks[18], (DEPTH, LRU_WIDTH), f32, minval=0.9, maxval=0.999)
    a0 = a8 ** (1.0 / LRU_C)
    lru_lambda = jnp.log(a0) - jnp.log1p(-a0)

    out_gain = gain(ks[19], (DEPTH, D_MIX))
    w_out = nrm(ks[20], (DEPTH, D_MIX, D_MODEL), D_MIX)

    ffn_w_gate = nrm(ks[21], (N_DENSE, D_MODEL, D_FF), D_MODEL)
    ffn_w_up = nrm(ks[22], (N_DENSE, D_MODEL, D_FF), D_MODEL)
    ffn_w_down = nrm(ks[23], (N_DENSE, D_FF, D_MODEL), D_FF)

    router_w = nrm(ks[24], (N_MOE, D_MODEL, N_EXPERTS), D_MODEL)
    moe_w_gate = nrm(ks[25], (N_MOE, N_EXPERTS, D_MODEL, D_FF_EXPERT), D_MODEL)
    moe_w_up = nrm(ks[26], (N_MOE, N_EXPERTS, D_MODEL, D_FF_EXPERT), D_MODEL)
    moe_w_down = nrm(ks[27], (N_MOE, N_EXPERTS, D_FF_EXPERT, D_MODEL), D_FF_EXPERT)

    return {"x": x, "c": c, "positions": positions, "w_ada": w_ada, "b_ada": b_ada,
            "norm_gain": norm_gain, "w_in": w_in, "latent_gain": latent_gain, "w_q_up": w_q_up,
            "w_kv_up": w_kv_up, "q_gain": q_gain, "k_gain": k_gain, "conv_w": conv_w, "conv_b": conv_b,
            "w_a": w_a, "b_a": b_a, "w_x": w_x, "b_x": b_x, "lru_lambda": lru_lambda,
            "out_gain": out_gain, "w_out": w_out, "ffn_w_gate": ffn_w_gate, "ffn_w_up": ffn_w_up,
            "ffn_w_down": ffn_w_down, "router_w": router_w, "moe_w_gate": moe_w_gate,
            "moe_w_up": moe_w_up, "moe_w_down": moe_w_down}


def reference(x, c, positions, w_ada, b_ada, norm_gain, w_in, latent_gain, w_q_up, w_kv_up, q_gain,
              k_gain, conv_w, conv_b, w_a, b_a, w_x, b_x, lru_lambda, out_gain, w_out, ffn_w_gate,
              ffn_w_up, ffn_w_down, router_w, moe_w_gate, moe_w_up, moe_w_down):
    c_act = jax.nn.silu(c)
    inv_freq = ROPE_THETA ** (-jnp.arange(0, ROPE_DIM, 2, dtype=jnp.float32) / ROPE_DIM)
    ang = positions.astype(jnp.float32)[..., None] * inv_freq
    cos = jnp.cos(ang).astype(x.dtype)
    sin = jnp.sin(ang).astype(x.dtype)
    for l in range(DEPTH):
        shift, scale, gate = _modulate(c_act, w_ada[l, 0], b_ada[l, 0])
        h = _rms(x, norm_gain[l, 0]) * (1.0 + scale) + shift
        x = x + gate * _hybrid_mixer(h, positions, cos, sin, w_in[l], latent_gain[l], w_q_up[l],
                                     w_kv_up[l], q_gain[l], k_gain[l], conv_w[l], conv_b[l], w_a[l],
                                     b_a[l], w_x[l], b_x[l], lru_lambda[l], out_gain[l], w_out[l])
        shift, scale, gate = _modulate(c_act, w_ada[l, 1], b_ada[l, 1])
        h = _rms(x, norm_gain[l, 1]) * (1.0 + scale) + shift
        if l % 2 == 0:
            j = l // 2
            f = _swiglu(h, ffn_w_gate[j], ffn_w_up[j], ffn_w_down[j])
        else:
            j = l // 2
            f = _moe(h, router_w[j], moe_w_gate[j], moe_w_up[j], moe_w_down[j])
        x = x + gate * f
    return x
```

```python
import functools

import numpy as np
import jax
import jax.numpy as jnp
from jax import lax
from jax.experimental import pallas as pl
from jax.experimental.pallas import tpu as pltpu

F32 = jnp.float32
BF16 = jnp.bfloat16

D_MODEL = 1024
DEPTH = 4
MLA_HEADS = 8
NOPE_DIM = 64
ROPE_DIM = 32
HALF_ROPE = ROPE_DIM // 2
V_DIM = 64
QK_DIM = NOPE_DIM + ROPE_DIM
Q_RANK = 256
KV_RANK = 128
MLA_WIDTH = MLA_HEADS * V_DIM
LRU_WIDTH = D_MODEL - MLA_WIDTH
LRU_BLOCKS = 8
LRU_BLOCK = LRU_WIDTH // LRU_BLOCKS
CONV_WIDTH = 4
LRU_C = 8.0
N_EXPERTS = 8
ROPE_THETA = 10000.0
EPS = 1e-6

HEAD_PAD = 128
QK_PAD = MLA_HEADS * HEAD_PAD
Z_COLS = Q_RANK + KV_RANK + HEAD_PAD + 2 * LRU_WIDTH
ROUTER_PAD = 128
SM_SCALE = QK_DIM ** -0.5
NEG = float(np.finfo(np.float32).min)
VMEM_LIMIT = 56 * 1024 * 1024


def _cparams(sem):
    return pltpu.CompilerParams(dimension_semantics=sem, vmem_limit_bytes=VMEM_LIMIT)


def _rms_rows(v):
    return v * lax.rsqrt(jnp.mean(v * v, axis=-1, keepdims=True) + EPS)


def _ada_kernel(c_ref, w_ref, b_ref, o_ref):
    c = c_ref[...]
    c_act = c * jax.nn.sigmoid(c)
    o_ref[0] = jnp.dot(c_act, w_ref[0], preferred_element_type=F32,
                       precision=lax.Precision.HIGHEST) + b_ref[0]


def _ada_all(c_pad, w_ada, b_ada):
    l2 = w_ada.shape[0]
    tn = 1536
    return pl.pallas_call(
        _ada_kernel,
        grid=(l2, 3 * D_MODEL // tn),
        in_specs=[pl.BlockSpec((8, D_MODEL), lambda l, n: (0, 0)),
                  pl.BlockSpec((1, D_MODEL, tn), lambda l, n: (l, 0, n)),
                  pl.BlockSpec((1, 1, tn), lambda l, n: (l, 0, n))],
        out_specs=pl.BlockSpec((1, 8, tn), lambda l, n: (l, 0, n)),
        out_shape=jax.ShapeDtypeStruct((l2, 8, 3 * D_MODEL), F32),
        compiler_params=_cparams(("parallel", "parallel")),
    )(c_pad, w_ada, b_ada)


def _rope_kernel(pos_ref, invf_ref, cos_ref, s1_ref, s2_ref):
    ang = pos_ref[0].astype(F32) * invf_ref[...]
    c = jnp.cos(ang)
    s = jnp.sin(ang)
    lane = lax.broadcasted_iota(jnp.int32, ang.shape, 1)
    first = (lane >= NOPE_DIM) & (lane < NOPE_DIM + HALF_ROPE)
    second = (lane >= NOPE_DIM + HALF_ROPE) & (lane < QK_DIM)
    cos_ref[0] = c
    s1_ref[0] = jnp.where(first, -s, 0.0)
    s2_ref[0] = jnp.where(second, s, 0.0)


def _rope_tables(pos3, invf, tm):
    b, s, _ = pos3.shape
    spec = pl.BlockSpec((1, tm, HEAD_PAD), lambda i, j: (i, j, 0))
    shp = jax.ShapeDtypeStruct((b, s, HEAD_PAD), F32)
    return pl.pallas_call(
        _rope_kernel,
        grid=(b, s // tm),
        in_specs=[pl.BlockSpec((1, tm, 1), lambda i, j: (i, j, 0)),
                  pl.BlockSpec((1, HEAD_PAD), lambda i, j: (0, 0))],
        out_specs=[spec, spec, spec],
        out_shape=[shp, shp, shp],
        compiler_params=_cparams(("parallel", "parallel")),
    )(pos3, invf)


def _premix_kernel(x_ref, cos_ref, s1_ref, s2_ref, shift_ref, scale_ref, ng_ref, wz_ref, lg_ref,
                   wq_ref, wk_ref, wv_ref, gm_ref, qg_ref, kg_ref, krg_ref,
                   q_out, k_out, v_out, xr_out, gr_out):
    x = x_ref[0]
    h = _rms_rows(x) * ng_ref[...]
    h = h * (1.0 + scale_ref[0]) + shift_ref[0]
    z = jnp.dot(h.astype(BF16), wz_ref[...], preferred_element_type=F32)
    xr_out[0] = z[:, Q_RANK + KV_RANK + HEAD_PAD:Q_RANK + KV_RANK + HEAD_PAD + LRU_WIDTH]
    gr_out[0] = z[:, Q_RANK + KV_RANK + HEAD_PAD + LRU_WIDTH:]

    lg = lg_ref[...]
    cq = _rms_rows(z[:, :Q_RANK]) * lg[:, :Q_RANK]
    ckv = (_rms_rows(z[:, Q_RANK:Q_RANK + KV_RANK]) * lg[:, Q_RANK:]).astype(BF16)
    kr = z[:, Q_RANK + KV_RANK:Q_RANK + KV_RANK + HEAD_PAD]

    qf = jnp.dot(cq.astype(BF16), wq_ref[...], preferred_element_type=F32)
    kf = jnp.dot(ckv, wk_ref[...], preferred_element_type=F32)
    v_out[0] = jnp.dot(ckv, wv_ref[...], preferred_element_type=F32).astype(BF16)

    cos = cos_ref[0]
    s1 = s1_ref[0]
    s2 = s2_ref[0]
    gm = gm_ref[...]

    def group_norm(blk):
        msq = jnp.dot((blk * blk).astype(BF16), gm, preferred_element_type=F32)
        return blk * lax.rsqrt(msq + EPS)

    def rope(blk):
        return (blk * cos + pltpu.roll(blk, HEAD_PAD - HALF_ROPE, 1) * s1
                + pltpu.roll(blk, HALF_ROPE, 1) * s2)

    kr_rot = rope(group_norm(kr) * krg_ref[...])
    qg = qg_ref[...]
    kg = kg_ref[...]
    for hd in range(MLA_HEADS):
        sl = slice(hd * HEAD_PAD, (hd + 1) * HEAD_PAD)
        qh = rope(group_norm(qf[:, sl]) * qg) * SM_SCALE
        q_out[0, :, sl] = qh.astype(BF16)
        kh = group_norm(kf[:, sl]) * kg + kr_rot
        k_out[0, :, sl] = kh.astype(BF16)


def _premix(x, cos, s1, s2, shift, scale, ng, wz, lgain, wq, wk, wv, gm, qg, kg, krg, tm):
    b, s, d = x.shape
    tok = lambda w: pl.BlockSpec((1, tm, w), lambda i, j: (i, j, 0))
    per_b = pl.BlockSpec((1, 1, d), lambda i, j: (i, 0, 0))
    full = lambda a: pl.BlockSpec(a.shape, lambda i, j: (0,) * a.ndim)
    return pl.pallas_call(
        _premix_kernel,
        grid=(b, s // tm),
        in_specs=[tok(d), tok(HEAD_PAD), tok(HEAD_PAD), tok(HEAD_PAD), per_b, per_b, full(ng), full(wz),
                  full(lgain), full(wq), full(wk), full(wv), full(gm), full(qg), full(kg), full(krg)],
        out_specs=[tok(QK_PAD), tok(QK_PAD), tok(QK_PAD), tok(LRU_WIDTH), tok(LRU_WIDTH)],
        out_shape=[jax.ShapeDtypeStruct((b, s, QK_PAD), BF16),
                   jax.ShapeDtypeStruct((b, s, QK_PAD), BF16),
                   jax.ShapeDtypeStruct((b, s, QK_PAD), BF16),
                   jax.ShapeDtypeStruct((b, s, LRU_WIDTH), F32),
                   jax.ShapeDtypeStruct((b, s, LRU_WIDTH), F32)],
        compiler_params=_cparams(("parallel", "parallel")),
    )(x, cos, s1, s2, shift, scale, ng, wz, lgain, wq, wk, wv, gm, qg, kg, krg)


def _attn_kernel(q_ref, k_ref, v_ref, pq_ref, pk_ref, o_ref, *, tq, tk):
    qi = pl.program_id(2)
    q_pair = (q_ref[0, :, 0:HEAD_PAD], q_ref[0, :, HEAD_PAD:2 * HEAD_PAD])
    pq = pq_ref[0]
    first = lax.broadcasted_iota(jnp.int32, (tq, HEAD_PAD), 1) < V_DIM

    def step(j, carry, masked):
        m0, l0, m1, l1, acc = carry
        ks = pl.multiple_of(j * tk, tk)
        kk = k_ref[0, pl.ds(ks, tk), :]
        vv = v_ref[0, pl.ds(ks, tk), :]
        if masked:
            keep = pk_ref[0, :, pl.ds(ks, tk)] <= pq
        res = []
        for hh, (m, l) in enumerate(((m0, l0), (m1, l1))):
            sl = slice(hh * HEAD_PAD, (hh + 1) * HEAD_PAD)
            s = lax.dot_general(q_pair[hh], kk[:, sl], (((1,), (1,)), ((), ())),
                                preferred_element_type=F32)
            if masked:
                s = jnp.where(keep, s, NEG)
            mn = jnp.maximum(m, jnp.max(s, axis=-1, keepdims=True))
            alpha = jnp.exp(m - mn)
            p = jnp.exp(s - mn)
            ln = alpha * l + jnp.sum(p, axis=-1, keepdims=True)
            pv = jnp.dot(p.astype(BF16), vv[:, sl], preferred_element_type=F32)
            res.append((mn, ln, alpha, pv))
        acc = acc * jnp.where(first, res[0][2], res[1][2]) + res[0][3] + res[1][3]
        return res[0][0], res[0][1], res[1][0], res[1][1], acc

    ratio = tq // tk
    init = (jnp.full((tq, 1), NEG, F32), jnp.zeros((tq, 1), F32),
            jnp.full((tq, 1), NEG, F32), jnp.zeros((tq, 1), F32),
            jnp.zeros((tq, HEAD_PAD), F32))
    carry = lax.fori_loop(0, qi * ratio, lambda j, c: step(j, c, False), init)
    for r in range(ratio):
        carry = step(qi * ratio + r, carry, True)
    _, l0, _, l1, acc = carry
    o_ref[0] = (acc / jnp.where(first, l0, l1)).astype(o_ref.dtype)


def _attention(q, k, v, pos_col, pos_row, tq, tk):
    b, s, _ = q.shape
    pairs = MLA_HEADS // 2
    return pl.pallas_call(
        functools.partial(_attn_kernel, tq=tq, tk=tk),
        grid=(b, pairs, s // tq),
        in_specs=[pl.BlockSpec((1, tq, 2 * HEAD_PAD), lambda i, p, j: (i, j, p)),
                  pl.BlockSpec((1, s, 2 * HEAD_PAD), lambda i, p, j: (i, 0, p)),
                  pl.BlockSpec((1, s, 2 * HEAD_PAD), lambda i, p, j: (i, 0, p)),
                  pl.BlockSpec((1, tq, 1), lambda i, p, j: (i, j, 0)),
                  pl.BlockSpec((1, 1, s), lambda i, p, j: (i, 0, 0))],
        out_specs=pl.BlockSpec((1, tq, 2 * V_DIM), lambda i, p, j: (i, j, p)),
        out_shape=jax.ShapeDtypeStruct((b, s, MLA_WIDTH), BF16),
        compiler_params=_cparams(("parallel", "parallel", "arbitrary")),
    )(q, k, v, pos_col, pos_row)


def _gelu_tanh(x):
    return 0.5 * x * (1.0 + jnp.tanh(np.sqrt(2.0 / np.pi).astype(np.float32) * (x + 0.044715 * (x * x * x))))


def _lru_kernel(xr_ref, gr_ref, cw_ref, cb_ref, wg_ref, ba_ref, bx_ref, lam_ref, o_ref, xbuf, hc, *, tm):
    ti = pl.program_id(1)
    halo = 8

    @pl.when(ti == 0)
    def _():
        xbuf[0:halo, :] = jnp.zeros((halo, LRU_WIDTH), F32)
        hc[...] = jnp.zeros_like(hc)

    @pl.when(ti > 0)
    def _():
        xbuf[0:halo, :] = xbuf[tm:tm + halo, :]

    xbuf[halo:tm + halo, :] = xr_ref[0]
    cw = cw_ref[...]
    xc = cb_ref[...] + cw[CONV_WIDTH - 1:CONV_WIDTH, :] * xbuf[halo:tm + halo, :]
    for tap in range(1, CONV_WIDTH):
        xc = xc + cw[CONV_WIDTH - 1 - tap:CONV_WIDTH - tap, :] * xbuf[halo - tap:tm + halo - tap, :]

    gz = jnp.dot(xc.astype(BF16), wg_ref[...], preferred_element_type=F32)
    r = jax.nn.sigmoid(gz[:, :LRU_WIDTH] + ba_ref[...])
    gi = jax.nn.sigmoid(gz[:, LRU_WIDTH:] + bx_ref[...])
    lam = lam_ref[...]
    log_sig = -(jnp.maximum(-lam, 0.0) + jnp.log1p(jnp.exp(-jnp.abs(lam))))
    log_a = LRU_C * r * log_sig
    a = jnp.exp(log_a)
    u = jnp.sqrt(-jnp.tanh(log_a) * (a * a + 1.0)) * (gi * xc)

    row = lax.broadcasted_iota(jnp.int32, (tm, LRU_WIDTH), 0)
    d = 1
    while d < tm:
        keep = row >= d
        a_prev = jnp.where(keep, pltpu.roll(a, d, 0), 1.0)
        u_prev = jnp.where(keep, pltpu.roll(u, d, 0), 0.0)
        u = a * u_prev + u
        a = a * a_prev
        d *= 2
    hseq = a * hc[...] + u
    hc[...] = hseq[tm - 1:tm, :]
    o_ref[0] = (hseq * _gelu_tanh(gr_ref[0])).astype(o_ref.dtype)


def _lru(xr, gr, cw, cb, wg, ba, bx, lam, tm):
    b, s, w = xr.shape
    tok = pl.BlockSpec((1, tm, w), lambda i, j: (i, j, 0))
    full = lambda a: pl.BlockSpec(a.shape, lambda i, j: (0,) * a.ndim)
    return pl.pallas_call(
        functools.partial(_lru_kernel, tm=tm),
        grid=(b, s // tm),
        in_specs=[tok, tok, full(cw), full(cb), full(wg), full(ba), full(bx), full(lam)],
        out_specs=tok,
        out_shape=jax.ShapeDtypeStruct((b, s, w), BF16),
        scratch_shapes=[pltpu.VMEM((tm + 8, w), F32), pltpu.VMEM((1, w), F32)],
        compiler_params=_cparams(("parallel", "arbitrary")),
    )(xr, gr, cw, cb, wg, ba, bx, lam)


def _postmix_kernel(*refs, router):
    if router:
        (a_ref, r_ref, x_ref, gate_ref, og_ref, wo_ref, ng_ref, shift_ref, scale_ref, wr_ref,
         x_out, h_out, comb_out) = refs
    else:
        (a_ref, r_ref, x_ref, gate_ref, og_ref, wo_ref, ng_ref, shift_ref, scale_ref,
         x_out, h_out) = refs
    og = og_ref[...]
    ya = (_rms_rows(a_ref[0].astype(F32)) * og[:, :MLA_WIDTH]).astype(BF16)
    yr = (_rms_rows(r_ref[0].astype(F32)) * og[:, MLA_WIDTH:]).astype(BF16)
    y = (jnp.dot(ya, wo_ref[0:MLA_WIDTH, :], preferred_element_type=F32)
         + jnp.dot(yr, wo_ref[MLA_WIDTH:, :], preferred_element_type=F32))
    xn = x_ref[0] + gate_ref[0] * y
    x_out[0] = xn
    h2 = _rms_rows(xn) * ng_ref[...]
    h2 = h2 * (1.0 + scale_ref[0]) + shift_ref[0]
    h_out[0] = h2.astype(BF16)
    if router:
        logits = jnp.dot(h2, wr_ref[...], preferred_element_type=F32, precision=lax.Precision.HIGHEST)
        lane = lax.broadcasted_iota(jnp.int32, logits.shape, 1)
        lg = jnp.where(lane < N_EXPERTS, logits, -jnp.inf)
        m1 = jnp.max(lg, axis=-1, keepdims=True)
        i1 = jnp.min(jnp.where(lg == m1, lane, ROUTER_PAD), axis=-1, keepdims=True)
        lg2 = jnp.where(lane == i1, -jnp.inf, lg)
        m2 = jnp.max(lg2, axis=-1, keepdims=True)
        i2 = jnp.min(jnp.where(lg2 == m2, lane, ROUTER_PAD), axis=-1, keepdims=True)
        e2 = jnp.exp(m2 - m1)
        den = 1.0 + e2
        comb_out[0] = jnp.where(lane == i1, 1.0 / den, 0.0) + jnp.where(lane == i2, e2 / den, 0.0)


def _postmix(attn, rec, x, gate, og, wo, ng, shift, scale, wr, tm):
    b, s, d = x.shape
    router = wr is not None
    tok = lambda w: pl.BlockSpec((1, tm, w), lambda i, j: (i, j, 0))
    per_b = pl.BlockSpec((1, 1, d), lambda i, j: (i, 0, 0))
    full = lambda a: pl.BlockSpec(a.shape, lambda i, j: (0,) * a.ndim)
    in_specs = [tok(MLA_WIDTH), tok(LRU_WIDTH), tok(d), per_b, full(og), full(wo), full(ng), per_b, per_b]
    args = [attn, rec, x, gate, og, wo, ng, shift, scale]
    out_specs = [tok(d), tok(d)]
    out_shape = [jax.ShapeDtypeStruct((b, s, d), F32), jax.ShapeDtypeStruct((b, s, d), BF16)]
    if router:
        in_specs.append(full(wr))
        args.append(wr)
        out_specs.append(tok(ROUTER_PAD))
        out_shape.append(jax.ShapeDtypeStruct((b, s, ROUTER_PAD), F32))
    return pl.pallas_call(
        functools.partial(_postmix_kernel, router=router),
        grid=(b, s // tm),
        in_specs=in_specs, out_specs=out_specs, out_shape=out_shape,
        compiler_params=_cparams(("parallel", "parallel")),
    )(*args)


def _ffn_kernel(*refs, n_e, n_f, weighted):
    if weighted:
        h_ref, wg_ref, wu_ref, wd_ref, x_ref, gate_ref, comb_ref, o_ref, acc = refs
    else:
        h_ref, wg_ref, wu_ref, wd_ref, x_ref, gate_ref, o_ref, acc = refs
    e = pl.program_id(1)
    f = pl.program_id(2)

    @pl.when((e == 0) & (f == 0))
    def _():
        acc[...] = jnp.zeros_like(acc)

    h = h_ref[...]
    g = jnp.dot(h, wg_ref[0], preferred_element_type=F32)
    u = jnp.dot(h, wu_ref[0], preferred_element_type=F32)
    act = (g * jax.nn.sigmoid(g)) * u
    if weighted:
        comb = comb_ref[...]
        lane = lax.broadcasted_iota(jnp.int32, comb.shape, 1)
        act = act * jnp.sum(jnp.where(lane == e, comb, 0.0), axis=-1, keepdims=True)
    acc[...] += jnp.dot(act.astype(BF16), wd_ref[0], preferred_element_type=F32)

    @pl.when((e == n_e - 1) & (f == n_f - 1))
    def _():
        o_ref[...] = x_ref[...] + gate_ref[0] * acc[...]


def _ffn(h, wg, wu, wd, x, gate, comb, seq, tm, tf):
    n, d = x.shape
    n_e, _, ff = wg.shape
    n_f = ff // tf
    weighted = comb is not None
    tiles_per_seq = seq // tm
    in_specs = [pl.BlockSpec((tm, d), lambda i, e, f: (i, 0)),
                pl.BlockSpec((1, d, tf), lambda i, e, f: (e, 0, f)),
                pl.BlockSpec((1, d, tf), lambda i, e, f: (e, 0, f)),
                pl.BlockSpec((1, tf, d), lambda i, e, f: (e, f, 0)),
                pl.BlockSpec((tm, d), lambda i, e, f: (i, 0)),
                pl.BlockSpec((1, 1, d), lambda i, e, f: (i // tiles_per_seq, 0, 0))]
    args = [h, wg, wu, wd, x, gate]
    if weighted:
        in_specs.append(pl.BlockSpec((tm, ROUTER_PAD), lambda i, e, f: (i, 0)))
        args.append(comb)
    return pl.pallas_call(
        functools.partial(_ffn_kernel, n_e=n_e, n_f=n_f, weighted=weighted),
        grid=(n // tm, n_e, n_f),
        in_specs=in_specs,
        out_specs=pl.BlockSpec((tm, d), lambda i, e, f: (i, 0)),
        out_shape=jax.ShapeDtypeStruct((n, d), F32),
        scratch_shapes=[pltpu.VMEM((tm, d), F32)],
        compiler_params=_cparams(("parallel", "arbitrary", "arbitrary")),
    )(*args)


def _group_mean_matrix():
    g = np.zeros((HEAD_PAD, HEAD_PAD), np.float32)
    g[:NOPE_DIM, :NOPE_DIM] = 1.0 / NOPE_DIM
    g[NOPE_DIM:QK_DIM, NOPE_DIM:QK_DIM] = 1.0 / ROPE_DIM
    return jnp.asarray(g, dtype=BF16)


def _pad_cols(a, left, total):
    return jnp.pad(a, ((0, 0), (left, total - left - a.shape[1])))


def _layer_weights(l, w_in, w_q_up, w_kv_up, q_gain, k_gain, w_a, w_x):
    wi = w_in[l]
    n_lat = Q_RANK + KV_RANK
    wz = jnp.concatenate([wi[:, :n_lat], _pad_cols(wi[:, n_lat:n_lat + ROPE_DIM], NOPE_DIM, HEAD_PAD),
                          wi[:, n_lat + ROPE_DIM:]], axis=1).astype(BF16)
    wq = jnp.pad(w_q_up[l].reshape(Q_RANK, MLA_HEADS, QK_DIM),
                 ((0, 0), (0, 0), (0, HEAD_PAD - QK_DIM))).reshape(Q_RANK, QK_PAD).astype(BF16)
    wkv = w_kv_up[l].reshape(KV_RANK, MLA_HEADS, NOPE_DIM + V_DIM)
    wk = jnp.pad(wkv[..., :NOPE_DIM], ((0, 0), (0, 0), (0, HEAD_PAD - NOPE_DIM)))
    wk = wk.reshape(KV_RANK, QK_PAD).astype(BF16)
    wv_pair = wkv[..., NOPE_DIM:].reshape(KV_RANK, MLA_HEADS // 2, 2, V_DIM)
    zeros = jnp.zeros_like(wv_pair[:, :, 0])
    wv = jnp.stack([jnp.concatenate([wv_pair[:, :, 0], zeros], -1),
                    jnp.concatenate([zeros, wv_pair[:, :, 1]], -1)], axis=2)
    wv = wv.reshape(KV_RANK, QK_PAD).astype(BF16)
    qg = _pad_cols(q_gain[l][None, :], 0, HEAD_PAD)
    kg = _pad_cols(k_gain[l][None, :NOPE_DIM], 0, HEAD_PAD)
    krg = _pad_cols(k_gain[l][None, NOPE_DIM:], NOPE_DIM, HEAD_PAD)
    eye = jnp.eye(LRU_BLOCKS, dtype=F32)
    blockdiag = lambda w: jnp.einsum('gij,gh->gihj', w, eye).reshape(LRU_WIDTH, LRU_WIDTH)
    wgate = jnp.concatenate([blockdiag(w_a[l]), blockdiag(w_x[l])], axis=1).astype(BF16)
    return wz, wq, wk, wv, qg, kg, krg, wgate


def kernel(x, c, positions, w_ada, b_ada, norm_gain, w_in, latent_gain, w_q_up, w_kv_up, q_gain, k_gain,
           conv_w, conv_b, w_a, b_a, w_x, b_x, lru_lambda, out_gain, w_out, ffn_w_gate, ffn_w_up,
           ffn_w_down, router_w, moe_w_gate, moe_w_up, moe_w_down):
    b, s, d = x.shape
    n = b * s
    tm = min(512, s)
    tq = min(512, s)
    tk = min(512, s)

    c_pad = jnp.pad(c, ((0, 8 - b), (0, 0)))
    mods = _ada_all(c_pad, w_ada.reshape(DEPTH * 2, d, 3 * d), b_ada.reshape(DEPTH * 2, 1, 3 * d))
    mods = mods[:, :b].reshape(DEPTH, 2, b, 1, 3, d)
    shift_of = lambda l, i: mods[l, i, :, :, 0]
    scale_of = lambda l, i: mods[l, i, :, :, 1]
    gate_of = lambda l, i: mods[l, i, :, :, 2]

    inv_freq = ROPE_THETA ** (-jnp.arange(0, ROPE_DIM, 2, dtype=F32) / ROPE_DIM)
    invf = _pad_cols(jnp.concatenate([inv_freq, inv_freq])[None, :], NOPE_DIM, HEAD_PAD)
    pos_col = positions.reshape(b, s, 1)
    pos_row = positions.reshape(b, 1, s)
    cos, s1, s2 = _rope_tables(pos_col, invf, tm)
    gm = _group_mean_matrix()

    for l in range(DEPTH):
        wz, wq, wk, wv, qg, kg, krg, wgate = _layer_weights(l, w_in, w_q_up, w_kv_up, q_gain, k_gain, w_a, w_x)
        q, k, v, xr, gr = _premix(x, cos, s1, s2, shift_of(l, 0), scale_of(l, 0), norm_gain[l, 0][None, :],
                                  wz, latent_gain[l][None, :], wq, wk, wv, gm, qg, kg, krg, tm)
        attn = _attention(q, k, v, pos_col, pos_row, tq, tk)
        rec = _lru(xr, gr, conv_w[l], conv_b[l][None, :], wgate, b_a[l][None, :], b_x[l][None, :],
                   lru_lambda[l][None, :], tm)
        j = l // 2
        moe = l % 2 == 1
        wr = jnp.pad(router_w[j], ((0, 0), (0, ROUTER_PAD - N_EXPERTS))) if moe else None
        outs = _postmix(attn, rec, x, gate_of(l, 0), out_gain[l][None, :], w_out[l].astype(BF16),
                        norm_gain[l, 1][None, :], shift_of(l, 1), scale_of(l, 1), wr, tm)
        x_mid, h2 = outs[0], outs[1]
        if moe:
            comb = outs[2].reshape(n, ROUTER_PAD)
            wg_, wu_, wd_ = moe_w_gate[j].astype(BF16), moe_w_up[j].astype(BF16), moe_w_down[j].astype(BF16)
            tf = 512
        else:
            comb = None
            wg_, wu_, wd_ = (ffn_w_gate[j][None].astype(BF16), ffn_w_up[j][None].astype(BF16),
                             ffn_w_down[j][None].astype(BF16))
            tf = 1408
        x = _ffn(h2.reshape(n, d), wg_, wu_, wd_, x_mid.reshape(n, d), gate_of(l, 1), comb, s, tm, tf)
        x = x.reshape(b, s, d)
    return x
```

```python
import functools

import numpy as np
import jax
import jax.numpy as jnp
from jax import lax
from jax.experimental import pallas as pl
from jax.experimental.pallas import tpu as pltpu

F32 = jnp.float32
BF16 = jnp.bfloat16

D_MODEL = 1024
DEPTH = 4
MLA_HEADS = 8
NOPE_DIM = 64
ROPE_DIM = 32
HALF_ROPE = ROPE_DIM // 2
V_DIM = 64
QK_DIM = NOPE_DIM + ROPE_DIM
Q_RANK = 256
KV_RANK = 128
MLA_WIDTH = MLA_HEADS * V_DIM
LRU_WIDTH = D_MODEL - MLA_WIDTH
LRU_BLOCKS = 8
LRU_BLOCK = LRU_WIDTH // LRU_BLOCKS
CONV_WIDTH = 4
LRU_C = 8.0
N_EXPERTS = 8
ROPE_THETA = 10000.0
EPS = 1e-6

HEAD_PAD = 128
QK_PAD = MLA_HEADS * HEAD_PAD
Z_COLS = Q_RANK + KV_RANK + HEAD_PAD + 2 * LRU_WIDTH
ROUTER_PAD = 128
SM_SCALE = QK_DIM ** -0.5
NEG = float(np.finfo(np.float32).min)
VMEM_LIMIT = 56 * 1024 * 1024


def _cparams(sem):
    return pltpu.CompilerParams(dimension_semantics=sem, vmem_limit_bytes=VMEM_LIMIT)


def _rms_rows(v):
    return v * lax.rsqrt(jnp.mean(v * v, axis=-1, keepdims=True) + EPS)


def _ada_kernel(c_ref, w_ref, b_ref, o_ref):
    c = c_ref[...]
    c_act = c * jax.nn.sigmoid(c)
    o_ref[0] = jnp.dot(c_act, w_ref[0], preferred_element_type=F32,
                       precision=lax.Precision.HIGHEST) + b_ref[0]


def _ada_all(c_pad, w_ada, b_ada):
    l2 = w_ada.shape[0]
    tn = 1536
    return pl.pallas_call(
        _ada_kernel,
        grid=(l2, 3 * D_MODEL // tn),
        in_specs=[pl.BlockSpec((8, D_MODEL), lambda l, n: (0, 0)),
                  pl.BlockSpec((1, D_MODEL, tn), lambda l, n: (l, 0, n)),
                  pl.BlockSpec((1, 1, tn), lambda l, n: (l, 0, n))],
        out_specs=pl.BlockSpec((1, 8, tn), lambda l, n: (l, 0, n)),
        out_shape=jax.ShapeDtypeStruct((l2, 8, 3 * D_MODEL), F32),
        compiler_params=_cparams(("parallel", "parallel")),
    )(c_pad, w_ada, b_ada)


def _rope_kernel(pos_ref, invf_ref, cos_ref, s1_ref, s2_ref):
    ang = pos_ref[0].astype(F32) * invf_ref[...]
    c = jnp.cos(ang)
    s = jnp.sin(ang)
    lane = lax.broadcasted_iota(jnp.int32, ang.shape, 1)
    first = (lane >= NOPE_DIM) & (lane < NOPE_DIM + HALF_ROPE)
    second = (lane >= NOPE_DIM + HALF_ROPE) & (lane < QK_DIM)
    cos_ref[0] = c
    s1_ref[0] = jnp.where(first, -s, 0.0)
    s2_ref[0] = jnp.where(second, s, 0.0)


def _rope_tables(pos3, invf, tm):
    b, s, _ = pos3.shape
    spec = pl.BlockSpec((1, tm, HEAD_PAD), lambda i, j: (i, j, 0))
    shp = jax.ShapeDtypeStruct((b, s, HEAD_PAD), F32)
    return pl.pallas_call(
        _rope_kernel,
        grid=(b, s // tm),
        in_specs=[pl.BlockSpec((1, tm, 1), lambda i, j: (i, j, 0)),
                  pl.BlockSpec((1, HEAD_PAD), lambda i, j: (0, 0))],
        out_specs=[spec, spec, spec],
        out_shape=[shp, shp, shp],
        compiler_params=_cparams(("parallel", "parallel")),
    )(pos3, invf)


def _premix_kernel(x_ref, cos_ref, s1_ref, s2_ref, shift_ref, scale_ref, ng_ref, wz_ref, lg_ref,
                   wq_ref, wk_ref, wv_ref, gm_ref, qg_ref, kg_ref, krg_ref,
                   q_out, k_out, v_out, xr_out, gr_out):
    x = x_ref[0]
    h = _rms_rows(x) * ng_ref[...]
    h = h * (1.0 + scale_ref[0]) + shift_ref[0]
    z = jnp.dot(h.astype(BF16), wz_ref[...], preferred_element_type=F32)
    xr_out[0] = z[:, Q_RANK + KV_RANK + HEAD_PAD:Q_RANK + KV_RANK + HEAD_PAD + LRU_WIDTH]
    gr_out[0] = z[:, Q_RANK + KV_RANK + HEAD_PAD + LRU_WIDTH:]

    lg = lg_ref[...]
    cq = _rms_rows(z[:, :Q_RANK]) * lg[:, :Q_RANK]
    ckv = (_rms_rows(z[:, Q_RANK:Q_RANK + KV_RANK]) * lg[:, Q_RANK:]).astype(BF16)
    kr = z[:, Q_RANK + KV_RANK:Q_RANK + KV_RANK + HEAD_PAD]

    qf = jnp.dot(cq.astype(BF16), wq_ref[...], preferred_element_type=F32)
    kf = jnp.dot(ckv, wk_ref[...], preferred_element_type=F32)
    vf = jnp.dot(ckv, wv_ref[...], preferred_element_type=F32)
    vlane = lax.broadcasted_iota(jnp.int32, vf.shape, 1)
    v_out[0] = jnp.where(vlane % HEAD_PAD == V_DIM, 1.0, vf).astype(BF16)

    cos = cos_ref[0]
    s1 = s1_ref[0]
    s2 = s2_ref[0]
    gm = gm_ref[...]

    def group_norm(blk):
        msq = jnp.dot((blk * blk).astype(BF16), gm, preferred_element_type=F32)
        return blk * lax.rsqrt(msq + EPS)

    def rope(blk):
        return (blk * cos + pltpu.roll(blk, HEAD_PAD - HALF_ROPE, 1) * s1
                + pltpu.roll(blk, HALF_ROPE, 1) * s2)

    kr_rot = rope(group_norm(kr) * krg_ref[...])
    qg = qg_ref[...]
    kg = kg_ref[...]
    for hd in range(MLA_HEADS):
        sl = slice(hd * HEAD_PAD, (hd + 1) * HEAD_PAD)
        qh = rope(group_norm(qf[:, sl]) * qg) * SM_SCALE
        q_out[0, :, sl] = qh.astype(BF16)
        kh = group_norm(kf[:, sl]) * kg + kr_rot
        k_out[0, :, sl] = kh.astype(BF16)


def _premix(x, cos, s1, s2, shift, scale, ng, wz, lgain, wq, wk, wv, gm, qg, kg, krg, tm):
    b, s, d = x.shape
    tok = lambda w: pl.BlockSpec((1, tm, w), lambda i, j: (i, j, 0))
    per_b = pl.BlockSpec((1, 1, d), lambda i, j: (i, 0, 0))
    full = lambda a: pl.BlockSpec(a.shape, lambda i, j: (0,) * a.ndim)
    return pl.pallas_call(
        _premix_kernel,
        grid=(b, s // tm),
        in_specs=[tok(d), tok(HEAD_PAD), tok(HEAD_PAD), tok(HEAD_PAD), per_b, per_b, full(ng), full(wz),
                  full(lgain), full(wq), full(wk), full(wv), full(gm), full(qg), full(kg), full(krg)],
        out_specs=[tok(QK_PAD), tok(QK_PAD), tok(QK_PAD), tok(LRU_WIDTH), tok(LRU_WIDTH)],
        out_shape=[jax.ShapeDtypeStruct((b, s, QK_PAD), BF16),
                   jax.ShapeDtypeStruct((b, s, QK_PAD), BF16),
                   jax.ShapeDtypeStruct((b, s, QK_PAD), BF16),
                   jax.ShapeDtypeStruct((b, s, LRU_WIDTH), F32),
                   jax.ShapeDtypeStruct((b, s, LRU_WIDTH), F32)],
        compiler_params=_cparams(("parallel", "parallel")),
    )(x, cos, s1, s2, shift, scale, ng, wz, lgain, wq, wk, wv, gm, qg, kg, krg)


ATTN_ROWS = 64
LANES = 128


def _attn_kernel(q_ref, k_ref, v_ref, pq_ref, pk_ref, o_ref, s_scr, p_scr, m_scr, acc_scr, pq_scr, *, tq, tk):
    assert tq == tk
    qi = pl.program_id(2)
    m_scr[...] = jnp.full(m_scr.shape, NEG, F32)
    acc_scr[...] = jnp.zeros(acc_scr.shape, F32)
    pq_scr[...] = jnp.broadcast_to(pq_ref[0], pq_scr.shape)

    def scores(j, slot):
        ks = pl.multiple_of(j * tk, tk)
        for hh in range(2):
            sl = slice(hh * HEAD_PAD, (hh + 1) * HEAD_PAD)
            s_scr[slot, hh] = lax.dot_general(q_ref[0, :, sl], k_ref[0, pl.ds(ks, tk), sl],
                                              (((1,), (1,)), ((), ())), preferred_element_type=F32)

    def softmax_pv(j, slot, masked):
        ks = pl.multiple_of(j * tk, tk)
        for hh in range(2):
            sl = slice(hh * HEAD_PAD, (hh + 1) * HEAD_PAD)
            for rb in range(tq // ATTN_ROWS):
                rows = slice(rb * ATTN_ROWS, (rb + 1) * ATTN_ROWS)
                blocks = []
                for cb in range(tk // LANES):
                    blk = s_scr[slot, hh, rows, cb * LANES:(cb + 1) * LANES]
                    if masked:
                        keep = pk_ref[0, :, pl.ds(ks + cb * LANES, LANES)] <= pq_scr[rows, :]
                        blk = jnp.where(keep, blk, NEG)
                    blocks.append(blk)
                rmax = blocks[0]
                for blk in blocks[1:]:
                    rmax = jnp.maximum(rmax, blk)
                m_old = m_scr[hh, rows, :]
                m_new = jnp.maximum(m_old, jnp.max(rmax, axis=-1, keepdims=True))
                m_scr[hh, rows, :] = m_new
                acc_scr[hh, rows, :] = acc_scr[hh, rows, :] * jnp.exp(m_old - m_new)
                for cb, blk in enumerate(blocks):
                    p_scr[hh, rows, cb * LANES:(cb + 1) * LANES] = jnp.exp(blk - m_new).astype(BF16)
            acc_scr[hh] += jnp.dot(p_scr[hh], v_ref[0, pl.ds(ks, tk), sl], preferred_element_type=F32)

    def body(j, carry):
        scores(j, 0)
        softmax_pv(j, 0, False)
        return carry

    lax.fori_loop(0, qi, body, 0)
    scores(qi, 1)
    softmax_pv(qi, 1, True)

    lane = lax.broadcasted_iota(jnp.int32, (tq, LANES), 1)
    outs = []
    for hh in range(2):
        acc = acc_scr[hh]
        den = jnp.sum(jnp.where(lane == V_DIM, acc, 0.0), axis=-1, keepdims=True)
        outs.append(acc / den)
    o_ref[0] = jnp.where(lane < V_DIM, outs[0], pltpu.roll(outs[1], V_DIM, 1)).astype(o_ref.dtype)


def _attention(q, k, v, pos_col, pos_row, tq, tk):
    b, s, _ = q.shape
    pairs = MLA_HEADS // 2
    return pl.pallas_call(
        functools.partial(_attn_kernel, tq=tq, tk=tk),
        grid=(b, pairs, s // tq),
        in_specs=[pl.BlockSpec((1, tq, 2 * HEAD_PAD), lambda i, p, j: (i, j, p)),
                  pl.BlockSpec((1, s, 2 * HEAD_PAD), lambda i, p, j: (i, 0, p)),
                  pl.BlockSpec((1, s, 2 * HEAD_PAD), lambda i, p, j: (i, 0, p)),
                  pl.BlockSpec((1, tq, 1), lambda i, p, j: (i, j, 0)),
                  pl.BlockSpec((1, 1, s), lambda i, p, j: (i, 0, 0))],
        out_specs=pl.BlockSpec((1, tq, 2 * V_DIM), lambda i, p, j: (i, j, p)),
        out_shape=jax.ShapeDtypeStruct((b, s, MLA_WIDTH), BF16),
        scratch_shapes=[pltpu.VMEM((2, 2, tq, tk), F32), pltpu.VMEM((2, tq, tk), BF16),
                        pltpu.VMEM((2, tq, LANES), F32), pltpu.VMEM((2, tq, LANES), F32),
                        pltpu.VMEM((tq, LANES), jnp.int32)],
        compiler_params=_cparams(("parallel", "parallel", "arbitrary")),
    )(q, k, v, pos_col, pos_row)


def _gelu_tanh(x):
    return 0.5 * x * (1.0 + jnp.tanh(np.sqrt(2.0 / np.pi).astype(np.float32) * (x + 0.044715 * (x * x * x))))


def _lru_kernel(xr_ref, gr_ref, cw_ref, cb_ref, wg_ref, ba_ref, bx_ref, lam_ref, o_ref, xbuf, hc, *, tm):
    ti = pl.program_id(1)
    halo = 8

    @pl.when(ti == 0)
    def _():
        xbuf[0:halo, :] = jnp.zeros((halo, LRU_WIDTH), F32)
        hc[...] = jnp.zeros_like(hc)

    @pl.when(ti > 0)
    def _():
        xbuf[0:halo, :] = xbuf[tm:tm + halo, :]

    xbuf[halo:tm + halo, :] = xr_ref[0]
    cw = cw_ref[...]
    xc = cb_ref[...] + cw[CONV_WIDTH - 1:CONV_WIDTH, :] * xbuf[halo:tm + halo, :]
    for tap in range(1, CONV_WIDTH):
        xc = xc + cw[CONV_WIDTH - 1 - tap:CONV_WIDTH - tap, :] * xbuf[halo - tap:tm + halo - tap, :]

    gz = jnp.dot(xc.astype(BF16), wg_ref[...], preferred_element_type=F32)
    r = jax.nn.sigmoid(gz[:, :LRU_WIDTH] + ba_ref[...])
    gi = jax.nn.sigmoid(gz[:, LRU_WIDTH:] + bx_ref[...])
    lam = lam_ref[...]
    log_sig = -(jnp.maximum(-lam, 0.0) + jnp.log1p(jnp.exp(-jnp.abs(lam))))
    log_a = LRU_C * r * log_sig
    a = jnp.exp(log_a)
    u = jnp.sqrt(-jnp.tanh(log_a) * (a * a + 1.0)) * (gi * xc)

    row = lax.broadcasted_iota(jnp.int32, (tm, LRU_WIDTH), 0)
    d = 1
    while d < tm:
        keep = row >= d
        a_prev = jnp.where(keep, pltpu.roll(a, d, 0), 1.0)
        u_prev = jnp.where(keep, pltpu.roll(u, d, 0), 0.0)
        u = a * u_prev + u
        a = a * a_prev
        d *= 2
    hseq = a * hc[...] + u
    hc[...] = hseq[tm - 1:tm, :]
    o_ref[0] = (hseq * _gelu_tanh(gr_ref[0])).astype(o_ref.dtype)


def _lru(xr, gr, cw, cb, wg, ba, bx, lam, tm):
    b, s, w = xr.shape
    tok = pl.BlockSpec((1, tm, w), lambda i, j: (i, j, 0))
    full = lambda a: pl.BlockSpec(a.shape, lambda i, j: (0,) * a.ndim)
    return pl.pallas_call(
        functools.partial(_lru_kernel, tm=tm),
        grid=(b, s // tm),
        in_specs=[tok, tok, full(cw), full(cb), full(wg), full(ba), full(bx), full(lam)],
        out_specs=tok,
        out_shape=jax.ShapeDtypeStruct((b, s, w), BF16),
        scratch_shapes=[pltpu.VMEM((tm + 8, w), F32), pltpu.VMEM((1, w), F32)],
        compiler_params=_cparams(("parallel", "arbitrary")),
    )(xr, gr, cw, cb, wg, ba, bx, lam)


ROUTE_I1, ROUTE_I2, ROUTE_W1, ROUTE_W2, ROUTE_R1, ROUTE_R2 = range(6)


def _postmix_kernel(*refs, router):
    if router:
        (a_ref, r_ref, x_ref, gate_ref, og_ref, wo_ref, ng_ref, shift_ref, scale_ref, wr_ref, tri_ref,
         x_out, h_out, route_out, count_out, count_scr) = refs
    else:
        (a_ref, r_ref, x_ref, gate_ref, og_ref, wo_ref, ng_ref, shift_ref, scale_ref,
         x_out, h_out) = refs
    og = og_ref[...]
    ya = (_rms_rows(a_ref[0].astype(F32)) * og[:, :MLA_WIDTH]).astype(BF16)
    yr = (_rms_rows(r_ref[0].astype(F32)) * og[:, MLA_WIDTH:]).astype(BF16)
    y = (jnp.dot(ya, wo_ref[0:MLA_WIDTH, :], preferred_element_type=F32)
         + jnp.dot(yr, wo_ref[MLA_WIDTH:, :], preferred_element_type=F32))
    xn = x_ref[0] + gate_ref[0] * y
    x_out[0] = xn
    h2 = _rms_rows(xn) * ng_ref[...]
    h2 = h2 * (1.0 + scale_ref[0]) + shift_ref[0]
    h_out[0] = h2.astype(h_out.dtype)
    if router:
        @pl.when((pl.program_id(0) == 0) & (pl.program_id(1) == 0))
        def _():
            count_scr[...] = jnp.zeros_like(count_scr)

        logits = jnp.dot(h2, wr_ref[...], preferred_element_type=F32, precision=lax.Precision.HIGHEST)
        lane = lax.broadcasted_iota(jnp.int32, logits.shape, 1)
        lg = jnp.where(lane < N_EXPERTS, logits, -jnp.inf)
        m1 = jnp.max(lg, axis=-1, keepdims=True)
        i1 = jnp.min(jnp.where(lg == m1, lane, ROUTER_PAD), axis=-1, keepdims=True)
        lg2 = jnp.where(lane == i1, -jnp.inf, lg)
        m2 = jnp.max(lg2, axis=-1, keepdims=True)
        i2 = jnp.min(jnp.where(lg2 == m2, lane, ROUTER_PAD), axis=-1, keepdims=True)
        e2 = jnp.exp(m2 - m1)
        den = 1.0 + e2
        sel1 = lane == i1
        sel2 = lane == i2
        sel = jnp.where(sel1, 1.0, jnp.where(sel2, 1.0, 0.0))
        before = count_scr[...] + jnp.dot(tri_ref[...], sel.astype(BF16), preferred_element_type=F32)
        r1 = jnp.sum(jnp.where(sel1, before, 0.0), axis=-1, keepdims=True)
        r2 = jnp.sum(jnp.where(sel2, before, 0.0), axis=-1, keepdims=True)
        count_scr[...] += jnp.sum(sel, axis=0, keepdims=True)
        count_out[...] = count_scr[...]
        rec = jnp.zeros(logits.shape, F32)
        for ln, val in ((ROUTE_I1, i1.astype(F32)), (ROUTE_I2, i2.astype(F32)), (ROUTE_W1, 1.0 / den),
                        (ROUTE_W2, e2 / den), (ROUTE_R1, r1), (ROUTE_R2, r2)):
            rec = jnp.where(lane == ln, val, rec)
        route_out[0] = rec


def _postmix(attn, rec, x, gate, og, wo, ng, shift, scale, wr, tm):
    b, s, d = x.shape
    router = wr is not None
    tok = lambda w: pl.BlockSpec((1, tm, w), lambda i, j: (i, j, 0))
    per_b = pl.BlockSpec((1, 1, d), lambda i, j: (i, 0, 0))
    full = lambda a: pl.BlockSpec(a.shape, lambda i, j: (0,) * a.ndim)
    in_specs = [tok(MLA_WIDTH), tok(LRU_WIDTH), tok(d), per_b, full(og), full(wo), full(ng), per_b, per_b]
    args = [attn, rec, x, gate, og, wo, ng, shift, scale]
    out_specs = [tok(d), tok(d)]
    out_shape = [jax.ShapeDtypeStruct((b, s, d), F32), jax.ShapeDtypeStruct((b, s, d), F32 if router else BF16)]
    scratch = []
    if router:
        tri = jnp.asarray(np.tril(np.ones((tm, tm), np.float32), -1), dtype=BF16)
        in_specs += [full(wr), full(tri)]
        args += [wr, tri]
        out_specs += [tok(ROUTER_PAD), pl.BlockSpec((1, ROUTER_PAD), lambda i, j: (0, 0))]
        out_shape += [jax.ShapeDtypeStruct((b, s, ROUTER_PAD), F32), jax.ShapeDtypeStruct((1, ROUTER_PAD), F32)]
        scratch = [pltpu.VMEM((1, ROUTER_PAD), F32)]
    return pl.pallas_call(
        functools.partial(_postmix_kernel, router=router),
        grid=(b, s // tm),
        in_specs=in_specs, out_specs=out_specs, out_shape=out_shape, scratch_shapes=scratch,
        compiler_params=_cparams(("arbitrary", "arbitrary") if router else ("parallel", "parallel")),
    )(*args)


def _swiglu_chunk(h, wg, wu, wd):
    g = jnp.dot(h, wg, preferred_element_type=F32)
    u = jnp.dot(h, wu, preferred_element_type=F32)
    act = (g * jax.nn.sigmoid(g)) * u
    return jnp.dot(act.astype(BF16), wd, preferred_element_type=F32)


def _ffn_kernel(h_ref, wg_ref, wu_ref, wd_ref, x_ref, gate_ref, o_ref, acc, *, n_f):
    f = pl.program_id(1)

    @pl.when(f == 0)
    def _():
        acc[...] = jnp.zeros_like(acc)

    acc[...] += _swiglu_chunk(h_ref[...], wg_ref[...], wu_ref[...], wd_ref[...])

    @pl.when(f == n_f - 1)
    def _():
        o_ref[...] = x_ref[...] + gate_ref[0] * acc[...]


def _ffn(h, wg, wu, wd, x, gate, seq, tm, tf):
    n, d = x.shape
    n_f = wg.shape[1] // tf
    tiles_per_seq = seq // tm
    return pl.pallas_call(
        functools.partial(_ffn_kernel, n_f=n_f),
        grid=(n // tm, n_f),
        in_specs=[pl.BlockSpec((tm, d), lambda i, f: (i, 0)),
                  pl.BlockSpec((d, tf), lambda i, f: (0, f)),
                  pl.BlockSpec((d, tf), lambda i, f: (0, f)),
                  pl.BlockSpec((tf, d), lambda i, f: (f, 0)),
                  pl.BlockSpec((tm, d), lambda i, f: (i, 0)),
                  pl.BlockSpec((1, 1, d), lambda i, f: (i // tiles_per_seq, 0, 0))],
        out_specs=pl.BlockSpec((tm, d), lambda i, f: (i, 0)),
        out_shape=jax.ShapeDtypeStruct((n, d), F32),
        scratch_shapes=[pltpu.VMEM((tm, d), F32)],
        compiler_params=_cparams(("parallel", "arbitrary")),
    )(h, wg, wu, wd, x, gate)


def _row_copies_wait(src_row, dst_row, sem, count):
    def body(_, c):
        pltpu.make_async_copy(src_row, dst_row, sem).wait()
        return c
    lax.fori_loop(0, count, body, 0)


def _dispatch_kernel(slot_ref, h_ref, init_ref, xs_ref, sem, *, tm):
    del init_ref

    def start(r, c):
        for k in range(2):
            dst = slot_ref[0, 0, 2 * r + k]
            pltpu.make_async_copy(h_ref.at[pl.ds(r, 1), :], xs_ref.at[pl.ds(dst, 1), :], sem).start()
        return c

    lax.fori_loop(0, tm, start, 0)
    _row_copies_wait(h_ref.at[pl.ds(0, 1), :], xs_ref.at[pl.ds(0, 1), :], sem, 2 * tm)


def _dispatch(slots, h, xs_init, tm):
    n, d = h.shape
    return pl.pallas_call(
        functools.partial(_dispatch_kernel, tm=tm),
        grid=(n // tm,),
        in_specs=[pl.BlockSpec((1, 1, 2 * tm), lambda i: (i, 0, 0), memory_space=pltpu.SMEM),
                  pl.BlockSpec((tm, d), lambda i: (i, 0)),
                  pl.BlockSpec(memory_space=pl.ANY)],
        out_specs=pl.BlockSpec(memory_space=pl.ANY),
        out_shape=jax.ShapeDtypeStruct(xs_init.shape, F32),
        scratch_shapes=[pltpu.SemaphoreType.DMA(())],
        input_output_aliases={2: 0},
        compiler_params=_cparams(("arbitrary",)),
    )(slots, h, xs_init)


def _expert_kernel(te_ref, nu_ref, x_ref, wg_ref, wu_ref, wd_ref, o_ref, acc, *, n_f):
    del te_ref
    i = pl.program_id(0)
    f = pl.program_id(1)

    @pl.when(f == 0)
    def _():
        acc[...] = jnp.zeros_like(acc)

    @pl.when(i < nu_ref[0])
    def _():
        acc[...] += _swiglu_chunk(x_ref[...].astype(BF16), wg_ref[0], wu_ref[0], wd_ref[0])

    @pl.when(f == n_f - 1)
    def _():
        o_ref[...] = acc[...]


def _experts(tile_expert, n_used, xs, wg, wu, wd, tm, tf):
    p, d = xs.shape
    n_f = wg.shape[2] // tf
    f_of = lambda i, f, nu: jnp.where(i < nu[0], f, n_f - 1)
    grid_spec = pltpu.PrefetchScalarGridSpec(
        num_scalar_prefetch=2,
        grid=(p // tm, n_f),
        in_specs=[pl.BlockSpec((tm, d), lambda i, f, te, nu: (jnp.minimum(i, nu[0] - 1), 0)),
                  pl.BlockSpec((1, d, tf), lambda i, f, te, nu: (te[i], 0, f_of(i, f, nu))),
                  pl.BlockSpec((1, d, tf), lambda i, f, te, nu: (te[i], 0, f_of(i, f, nu))),
                  pl.BlockSpec((1, tf, d), lambda i, f, te, nu: (te[i], f_of(i, f, nu), 0))],
        out_specs=pl.BlockSpec((tm, d), lambda i, f, te, nu: (i, 0)),
        scratch_shapes=[pltpu.VMEM((tm, d), F32)])
    return pl.pallas_call(
        functools.partial(_expert_kernel, n_f=n_f),
        grid_spec=grid_spec,
        out_shape=jax.ShapeDtypeStruct((p, d), F32),
        compiler_params=_cparams(("arbitrary", "arbitrary")),
    )(tile_expert, n_used, xs, wg, wu, wd)


def _combine_kernel(slot_ref, ys_ref, route_ref, x_ref, gate_ref, o_ref, buf, sem, *, tm):
    def start(r, c):
        for k in range(2):
            src = slot_ref[0, 0, 2 * r + k]
            pltpu.make_async_copy(ys_ref.at[pl.ds(src, 1), :], buf.at[k, pl.ds(r, 1), :], sem).start()
        return c

    lax.fori_loop(0, tm, start, 0)
    _row_copies_wait(ys_ref.at[pl.ds(0, 1), :], buf.at[0, pl.ds(0, 1), :], sem, 2 * tm)
    route = route_ref[...]
    w1 = route[:, ROUTE_W1:ROUTE_W1 + 1]
    w2 = route[:, ROUTE_W2:ROUTE_W2 + 1]
    o_ref[...] = x_ref[...] + gate_ref[0] * (w1 * buf[0] + w2 * buf[1])


def _combine(slots, ys, route, x, gate, seq, tm):
    n, d = x.shape
    tiles_per_seq = seq // tm
    return pl.pallas_call(
        functools.partial(_combine_kernel, tm=tm),
        grid=(n // tm,),
        in_specs=[pl.BlockSpec((1, 1, 2 * tm), lambda i: (i, 0, 0), memory_space=pltpu.SMEM),
                  pl.BlockSpec(memory_space=pl.ANY),
                  pl.BlockSpec((tm, ROUTER_PAD), lambda i: (i, 0)),
                  pl.BlockSpec((tm, d), lambda i: (i, 0)),
                  pl.BlockSpec((1, 1, d), lambda i: (i // tiles_per_seq, 0, 0))],
        out_specs=pl.BlockSpec((tm, d), lambda i: (i, 0)),
        out_shape=jax.ShapeDtypeStruct((n, d), F32),
        scratch_shapes=[pltpu.VMEM((2, tm, d), F32), pltpu.SemaphoreType.DMA(())],
        compiler_params=_cparams(("arbitrary",)),
    )(slots, ys, route, x, gate)


def _moe(h, route, counts, wg, wu, wd, x, gate, seq, tm, tm_e, tf):
    n, d = x.shape
    cnt = counts[0, :N_EXPERTS].astype(jnp.int32)
    tiles = (cnt + tm_e - 1) // tm_e
    tile_end = jnp.cumsum(tiles)
    row_start = (tile_end - tiles) * tm_e
    n_tiles = 2 * n // tm_e + N_EXPERTS
    tile_expert = jnp.minimum(jnp.sum(jnp.arange(n_tiles)[:, None] >= tile_end[None, :], axis=1),
                              N_EXPERTS - 1).astype(jnp.int32)
    n_used = tile_end[-1:].astype(jnp.int32)
    expert = route[:, ROUTE_I1:ROUTE_I2 + 1].astype(jnp.int32)
    rank = route[:, ROUTE_R1:ROUTE_R2 + 1].astype(jnp.int32)
    slots = (row_start[expert] + rank).reshape(n // tm, 1, 2 * tm)
    xs = _dispatch(slots, h, jnp.zeros((n_tiles * tm_e, d), F32), tm)
    ys = _experts(tile_expert, n_used, xs, wg, wu, wd, tm_e, tf)
    return _combine(slots, ys, route, x, gate, seq, tm)


def _group_mean_matrix():
    g = np.zeros((HEAD_PAD, HEAD_PAD), np.float32)
    g[:NOPE_DIM, :NOPE_DIM] = 1.0 / NOPE_DIM
    g[NOPE_DIM:QK_DIM, NOPE_DIM:QK_DIM] = 1.0 / ROPE_DIM
    return jnp.asarray(g, dtype=BF16)


def _pad_cols(a, left, total):
    return jnp.pad(a, ((0, 0), (left, total - left - a.shape[1])))


def _layer_weights(l, w_in, w_q_up, w_kv_up, q_gain, k_gain, w_a, w_x):
    wi = w_in[l]
    n_lat = Q_RANK + KV_RANK
    wz = jnp.concatenate([wi[:, :n_lat], _pad_cols(wi[:, n_lat:n_lat + ROPE_DIM], NOPE_DIM, HEAD_PAD),
                          wi[:, n_lat + ROPE_DIM:]], axis=1).astype(BF16)
    wq = jnp.pad(w_q_up[l].reshape(Q_RANK, MLA_HEADS, QK_DIM),
                 ((0, 0), (0, 0), (0, HEAD_PAD - QK_DIM))).reshape(Q_RANK, QK_PAD).astype(BF16)
    wkv = w_kv_up[l].reshape(KV_RANK, MLA_HEADS, NOPE_DIM + V_DIM)
    wk = jnp.pad(wkv[..., :NOPE_DIM], ((0, 0), (0, 0), (0, HEAD_PAD - NOPE_DIM)))
    wk = wk.reshape(KV_RANK, QK_PAD).astype(BF16)
    wv = jnp.pad(wkv[..., NOPE_DIM:], ((0, 0), (0, 0), (0, HEAD_PAD - V_DIM)))
    wv = wv.reshape(KV_RANK, QK_PAD).astype(BF16)
    qg = _pad_cols(q_gain[l][None, :], 0, HEAD_PAD)
    kg = _pad_cols(k_gain[l][None, :NOPE_DIM], 0, HEAD_PAD)
    krg = _pad_cols(k_gain[l][None, NOPE_DIM:], NOPE_DIM, HEAD_PAD)
    eye = jnp.eye(LRU_BLOCKS, dtype=F32)
    blockdiag = lambda w: jnp.einsum('gij,gh->gihj', w, eye).reshape(LRU_WIDTH, LRU_WIDTH)
    wgate = jnp.concatenate([blockdiag(w_a[l]), blockdiag(w_x[l])], axis=1).astype(BF16)
    return wz, wq, wk, wv, qg, kg, krg, wgate


def kernel(x, c, positions, w_ada, b_ada, norm_gain, w_in, latent_gain, w_q_up, w_kv_up, q_gain, k_gain,
           conv_w, conv_b, w_a, b_a, w_x, b_x, lru_lambda, out_gain, w_out, ffn_w_gate, ffn_w_up,
           ffn_w_down, router_w, moe_w_gate, moe_w_up, moe_w_down):
    b, s, d = x.shape
    n = b * s
    tm = min(512, s)
    tq = min(512, s)
    tk = min(512, s)

    c_pad = jnp.pad(c, ((0, 8 - b), (0, 0)))
    mods = _ada_all(c_pad, w_ada.reshape(DEPTH * 2, d, 3 * d), b_ada.reshape(DEPTH * 2, 1, 3 * d))
    mods = mods[:, :b].reshape(DEPTH, 2, b, 1, 3, d)
    shift_of = lambda l, i: mods[l, i, :, :, 0]
    scale_of = lambda l, i: mods[l, i, :, :, 1]
    gate_of = lambda l, i: mods[l, i, :, :, 2]

    inv_freq = ROPE_THETA ** (-jnp.arange(0, ROPE_DIM, 2, dtype=F32) / ROPE_DIM)
    invf = _pad_cols(jnp.concatenate([inv_freq, inv_freq])[None, :], NOPE_DIM, HEAD_PAD)
    pos_col = positions.reshape(b, s, 1)
    pos_row = positions.reshape(b, 1, s)
    cos, s1, s2 = _rope_tables(pos_col, invf, tm)
    gm = _group_mean_matrix()

    for l in range(DEPTH):
        wz, wq, wk, wv, qg, kg, krg, wgate = _layer_weights(l, w_in, w_q_up, w_kv_up, q_gain, k_gain, w_a, w_x)
        q, k, v, xr, gr = _premix(x, cos, s1, s2, shift_of(l, 0), scale_of(l, 0), norm_gain[l, 0][None, :],
                                  wz, latent_gain[l][None, :], wq, wk, wv, gm, qg, kg, krg, tm)
        attn = _attention(q, k, v, pos_col, pos_row, tq, tk)
        rec = _lru(xr, gr, conv_w[l], conv_b[l][None, :], wgate, b_a[l][None, :], b_x[l][None, :],
                   lru_lambda[l][None, :], tm)
        j = l // 2
        moe = l % 2 == 1
        wr = jnp.pad(router_w[j], ((0, 0), (0, ROUTER_PAD - N_EXPERTS))) if moe else None
        outs = _postmix(attn, rec, x, gate_of(l, 0), out_gain[l][None, :], w_out[l].astype(BF16),
                        norm_gain[l, 1][None, :], shift_of(l, 1), scale_of(l, 1), wr, tm)
        x_mid, h2 = outs[0].reshape(n, d), outs[1].reshape(n, d)
        if moe:
            x = _moe(h2, outs[2].reshape(n, ROUTER_PAD), outs[3], moe_w_gate[j].astype(BF16),
                     moe_w_up[j].astype(BF16), moe_w_down[j].astype(BF16), x_mid, gate_of(l, 1),
                     s, tm, min(512, n), 512)
        else:
            x = _ffn(h2, ffn_w_gate[j].astype(BF16), ffn_w_up[j].astype(BF16), ffn_w_down[j].astype(BF16),
                     x_mid, gate_of(l, 1), s, tm, 1408)
        x = x.reshape(b, s, d)
    return x
```

```python
import functools

import numpy as np
import jax
import jax.numpy as jnp
from jax import lax
from jax.experimental import pallas as pl
from jax.experimental.pallas import tpu as pltpu

F32 = jnp.float32
BF16 = jnp.bfloat16

D_MODEL = 1024
DEPTH = 4
MLA_HEADS = 8
NOPE_DIM = 64
ROPE_DIM = 32
HALF_ROPE = ROPE_DIM // 2
V_DIM = 64
QK_DIM = NOPE_DIM + ROPE_DIM
Q_RANK = 256
KV_RANK = 128
MLA_WIDTH = MLA_HEADS * V_DIM
LRU_WIDTH = D_MODEL - MLA_WIDTH
LRU_BLOCKS = 8
LRU_BLOCK = LRU_WIDTH // LRU_BLOCKS
CONV_WIDTH = 4
LRU_C = 8.0
N_EXPERTS = 8
ROPE_THETA = 10000.0
EPS = 1e-6

HEAD_PAD = 128
QK_PAD = MLA_HEADS * HEAD_PAD
Z_COLS = Q_RANK + KV_RANK + HEAD_PAD + 2 * LRU_WIDTH
ROUTER_PAD = 128
SM_SCALE = QK_DIM ** -0.5 * float(np.log2(np.e))
NEG = float(np.finfo(np.float32).min)
VMEM_LIMIT = 56 * 1024 * 1024


def _cparams(sem):
    return pltpu.CompilerParams(dimension_semantics=sem, vmem_limit_bytes=VMEM_LIMIT)


def _rms_rows(v):
    return v * lax.rsqrt(jnp.mean(v * v, axis=-1, keepdims=True) + EPS)


def _ada_kernel(c_ref, w_ref, b_ref, o_ref):
    c = c_ref[...]
    c_act = c * jax.nn.sigmoid(c)
    o_ref[0] = jnp.dot(c_act, w_ref[0], preferred_element_type=F32,
                       precision=lax.Precision.HIGHEST) + b_ref[0]


def _ada_all(c_pad, w_ada, b_ada):
    l2 = w_ada.shape[0]
    tn = 1536
    return pl.pallas_call(
        _ada_kernel,
        grid=(l2, 3 * D_MODEL // tn),
        in_specs=[pl.BlockSpec((8, D_MODEL), lambda l, n: (0, 0)),
                  pl.BlockSpec((1, D_MODEL, tn), lambda l, n: (l, 0, n)),
                  pl.BlockSpec((1, 1, tn), lambda l, n: (l, 0, n))],
        out_specs=pl.BlockSpec((1, 8, tn), lambda l, n: (l, 0, n)),
        out_shape=jax.ShapeDtypeStruct((l2, 8, 3 * D_MODEL), F32),
        compiler_params=_cparams(("parallel", "parallel")),
    )(c_pad, w_ada, b_ada)


def _rope_kernel(pos_ref, invf_ref, cos_ref, s1_ref, s2_ref):
    ang = pos_ref[0].astype(F32) * invf_ref[...]
    c = jnp.cos(ang)
    s = jnp.sin(ang)
    lane = lax.broadcasted_iota(jnp.int32, ang.shape, 1)
    first = (lane >= NOPE_DIM) & (lane < NOPE_DIM + HALF_ROPE)
    second = (lane >= NOPE_DIM + HALF_ROPE) & (lane < QK_DIM)
    cos_ref[0] = c
    s1_ref[0] = jnp.where(first, -s, 0.0)
    s2_ref[0] = jnp.where(second, s, 0.0)


def _rope_tables(pos3, invf, tm):
    b, s, _ = pos3.shape
    spec = pl.BlockSpec((1, tm, HEAD_PAD), lambda i, j: (i, j, 0))
    shp = jax.ShapeDtypeStruct((b, s, HEAD_PAD), F32)
    return pl.pallas_call(
        _rope_kernel,
        grid=(b, s // tm),
        in_specs=[pl.BlockSpec((1, tm, 1), lambda i, j: (i, j, 0)),
                  pl.BlockSpec((1, HEAD_PAD), lambda i, j: (0, 0))],
        out_specs=[spec, spec, spec],
        out_shape=[shp, shp, shp],
        compiler_params=_cparams(("parallel", "parallel")),
    )(pos3, invf)


def _premix_kernel(x_ref, cos_ref, s1_ref, s2_ref, shift_ref, scale_ref, ng_ref, wz_ref, lg_ref,
                   wq_ref, wk_ref, wv_ref, gm_ref, qg_ref, kg_ref, krg_ref,
                   q_out, k_out, v_out, xr_out, gr_out):
    x = x_ref[0]
    h = _rms_rows(x) * ng_ref[...]
    h = h * (1.0 + scale_ref[0]) + shift_ref[0]
    z = jnp.dot(h.astype(BF16), wz_ref[...], preferred_element_type=F32)
    xr_out[0] = z[:, Q_RANK + KV_RANK + HEAD_PAD:Q_RANK + KV_RANK + HEAD_PAD + LRU_WIDTH]
    gr_out[0] = z[:, Q_RANK + KV_RANK + HEAD_PAD + LRU_WIDTH:]

    lg = lg_ref[...]
    cq = _rms_rows(z[:, :Q_RANK]) * lg[:, :Q_RANK]
    ckv = (_rms_rows(z[:, Q_RANK:Q_RANK + KV_RANK]) * lg[:, Q_RANK:]).astype(BF16)
    kr = z[:, Q_RANK + KV_RANK:Q_RANK + KV_RANK + HEAD_PAD]

    qf = jnp.dot(cq.astype(BF16), wq_ref[...], preferred_element_type=F32)
    kf = jnp.dot(ckv, wk_ref[...], preferred_element_type=F32)
    vf = jnp.dot(ckv, wv_ref[...], preferred_element_type=F32)
    vlane = lax.broadcasted_iota(jnp.int32, vf.shape, 1)
    v_out[0] = jnp.where(vlane % HEAD_PAD == V_DIM, 1.0, vf).astype(BF16)

    cos = cos_ref[0]
    s1 = s1_ref[0]
    s2 = s2_ref[0]
    gm = gm_ref[...]

    def group_norm(blk):
        msq = jnp.dot((blk * blk).astype(BF16), gm, preferred_element_type=F32)
        return blk * lax.rsqrt(msq + EPS)

    def rope(blk):
        return (blk * cos + pltpu.roll(blk, HEAD_PAD - HALF_ROPE, 1) * s1
                + pltpu.roll(blk, HALF_ROPE, 1) * s2)

    kr_rot = rope(group_norm(kr) * krg_ref[...])
    qg = qg_ref[...]
    kg = kg_ref[...]
    for hd in range(MLA_HEADS):
        sl = slice(hd * HEAD_PAD, (hd + 1) * HEAD_PAD)
        qh = rope(group_norm(qf[:, sl]) * qg) * SM_SCALE
        q_out[0, :, sl] = qh.astype(BF16)
        kh = group_norm(kf[:, sl]) * kg + kr_rot
        k_out[0, :, sl] = kh.astype(BF16)


def _premix(x, cos, s1, s2, shift, scale, ng, wz, lgain, wq, wk, wv, gm, qg, kg, krg, tm):
    b, s, d = x.shape
    tok = lambda w: pl.BlockSpec((1, tm, w), lambda i, j: (i, j, 0))
    per_b = pl.BlockSpec((1, 1, d), lambda i, j: (i, 0, 0))
    full = lambda a: pl.BlockSpec(a.shape, lambda i, j: (0,) * a.ndim)
    return pl.pallas_call(
        _premix_kernel,
        grid=(b, s // tm),
        in_specs=[tok(d), tok(HEAD_PAD), tok(HEAD_PAD), tok(HEAD_PAD), per_b, per_b, full(ng), full(wz),
                  full(lgain), full(wq), full(wk), full(wv), full(gm), full(qg), full(kg), full(krg)],
        out_specs=[tok(QK_PAD), tok(QK_PAD), tok(QK_PAD), tok(LRU_WIDTH), tok(LRU_WIDTH)],
        out_shape=[jax.ShapeDtypeStruct((b, s, QK_PAD), BF16),
                   jax.ShapeDtypeStruct((b, s, QK_PAD), BF16),
                   jax.ShapeDtypeStruct((b, s, QK_PAD), BF16),
                   jax.ShapeDtypeStruct((b, s, LRU_WIDTH), F32),
                   jax.ShapeDtypeStruct((b, s, LRU_WIDTH), F32)],
        compiler_params=_cparams(("parallel", "parallel")),
    )(x, cos, s1, s2, shift, scale, ng, wz, lgain, wq, wk, wv, gm, qg, kg, krg)


ATTN_ROWS = 32
ATTN_HEADS = 4
LANES = 128


def _attn_kernel(q_ref, k_ref, v_ref, pq_ref, pk_ref, o_ref, s_scr, p_scr, m_scr, acc_scr, pq_scr, *, tq, tk):
    assert tq == tk
    qi = pl.program_id(2)
    m_scr[...] = jnp.full(m_scr.shape, NEG, F32)
    acc_scr[...] = jnp.zeros(acc_scr.shape, F32)
    pq_scr[...] = jnp.broadcast_to(pq_ref[0], pq_scr.shape)
    head_lanes = [slice(hh * HEAD_PAD, (hh + 1) * HEAD_PAD) for hh in range(ATTN_HEADS)]

    def chunk(j, width, masked):
        ks = pl.multiple_of(j * tk, tk)
        for hh, sl in enumerate(head_lanes):
            s_scr[hh, :, 0:width] = lax.dot_general(q_ref[0, :, sl], k_ref[0, pl.ds(ks, width), sl],
                                                    (((1,), (1,)), ((), ())), preferred_element_type=F32)
        for hh, sl in enumerate(head_lanes):
            for rb in range(tq // ATTN_ROWS):
                rows = slice(rb * ATTN_ROWS, (rb + 1) * ATTN_ROWS)

                def block(cb):
                    blk = s_scr[hh, rows, cb * LANES:(cb + 1) * LANES]
                    if masked:
                        keep = pk_ref[0, :, pl.ds(ks + cb * LANES, LANES)] <= pq_scr[rows, :]
                        blk = jnp.where(keep, blk, NEG)
                    return blk

                rmax = block(0)
                for cb in range(1, width // LANES):
                    rmax = jnp.maximum(rmax, block(cb))
                m_old = m_scr[hh, rows, :]
                m_new = jnp.maximum(m_old, jnp.max(rmax, axis=-1, keepdims=True))
                m_scr[hh, rows, :] = m_new
                acc_scr[hh, rows, :] = acc_scr[hh, rows, :] * jnp.exp2(m_old - m_new)
                for cb in range(width // LANES):
                    p_scr[hh, rows, cb * LANES:(cb + 1) * LANES] = jnp.exp2(block(cb) - m_new).astype(BF16)
            acc_scr[hh] += jnp.dot(p_scr[hh, :, 0:width], v_ref[0, pl.ds(ks, width), sl],
                                   preferred_element_type=F32)

    def two_chunks(i, carry):
        chunk(2 * i, 2 * tk, False)
        return carry

    lax.fori_loop(0, qi // 2, two_chunks, 0)

    @pl.when(qi % 2 == 1)
    def _():
        chunk(qi - 1, tk, False)

    chunk(qi, tk, True)

    lane = lax.broadcasted_iota(jnp.int32, (tq, LANES), 1)
    for pair in range(ATTN_HEADS // 2):
        outs = []
        for hh in (2 * pair, 2 * pair + 1):
            acc = acc_scr[hh]
            den = jnp.sum(jnp.where(lane == V_DIM, acc, 0.0), axis=-1, keepdims=True)
            outs.append(acc / den)
        o_ref[0, :, pair * LANES:(pair + 1) * LANES] = jnp.where(
            lane < V_DIM, outs[0], pltpu.roll(outs[1], V_DIM, 1)).astype(o_ref.dtype)


def _attention(q, k, v, pos_col, pos_row, tq, tk):
    b, s, _ = q.shape
    groups = MLA_HEADS // ATTN_HEADS
    width = ATTN_HEADS * HEAD_PAD
    resident = lambda: pl.BlockSpec((1, s, width), lambda i, g, j: (i, 0, g), pipeline_mode=pl.Buffered(1))
    return pl.pallas_call(
        functools.partial(_attn_kernel, tq=tq, tk=tk),
        grid=(b, groups, s // tq),
        in_specs=[pl.BlockSpec((1, tq, width), lambda i, g, j: (i, j, g)),
                  resident(), resident(),
                  pl.BlockSpec((1, tq, 1), lambda i, g, j: (i, j, 0)),
                  pl.BlockSpec((1, 1, s), lambda i, g, j: (i, 0, 0))],
        out_specs=pl.BlockSpec((1, tq, ATTN_HEADS * V_DIM), lambda i, g, j: (i, j, g)),
        out_shape=jax.ShapeDtypeStruct((b, s, MLA_WIDTH), BF16),
        scratch_shapes=[pltpu.VMEM((ATTN_HEADS, tq, 2 * tk), F32), pltpu.VMEM((ATTN_HEADS, tq, 2 * tk), BF16),
                        pltpu.VMEM((ATTN_HEADS, tq, LANES), F32), pltpu.VMEM((ATTN_HEADS, tq, LANES), F32),
                        pltpu.VMEM((tq, LANES), jnp.int32)],
        compiler_params=_cparams(("parallel", "parallel", "arbitrary")),
    )(q, k, v, pos_col, pos_row)


def _gelu_tanh(x):
    return 0.5 * x * (1.0 + jnp.tanh(np.sqrt(2.0 / np.pi).astype(np.float32) * (x + 0.044715 * (x * x * x))))


def _lru_kernel(xr_ref, gr_ref, cw_ref, cb_ref, wg_ref, ba_ref, bx_ref, lam_ref, o_ref, xbuf, hc, *, tm):
    ti = pl.program_id(1)
    halo = 8

    @pl.when(ti == 0)
    def _():
        xbuf[0:halo, :] = jnp.zeros((halo, LRU_WIDTH), F32)
        hc[...] = jnp.zeros_like(hc)

    @pl.when(ti > 0)
    def _():
        xbuf[0:halo, :] = xbuf[tm:tm + halo, :]

    xbuf[halo:tm + halo, :] = xr_ref[0]
    cw = cw_ref[...]
    xc = cb_ref[...] + cw[CONV_WIDTH - 1:CONV_WIDTH, :] * xbuf[halo:tm + halo, :]
    for tap in range(1, CONV_WIDTH):
        xc = xc + cw[CONV_WIDTH - 1 - tap:CONV_WIDTH - tap, :] * xbuf[halo - tap:tm + halo - tap, :]

    gz = jnp.dot(xc.astype(BF16), wg_ref[...], preferred_element_type=F32)
    r = jax.nn.sigmoid(gz[:, :LRU_WIDTH] + ba_ref[...])
    gi = jax.nn.sigmoid(gz[:, LRU_WIDTH:] + bx_ref[...])
    lam = lam_ref[...]
    log_sig = -(jnp.maximum(-lam, 0.0) + jnp.log1p(jnp.exp(-jnp.abs(lam))))
    log_a = LRU_C * r * log_sig
    a = jnp.exp(log_a)
    u = jnp.sqrt(-jnp.tanh(log_a) * (a * a + 1.0)) * (gi * xc)

    row = lax.broadcasted_iota(jnp.int32, (tm, LRU_WIDTH), 0)
    d = 1
    while d < tm:
        keep = row >= d
        a_prev = jnp.where(keep, pltpu.roll(a, d, 0), 1.0)
        u_prev = jnp.where(keep, pltpu.roll(u, d, 0), 0.0)
        u = a * u_prev + u
        a = a * a_prev
        d *= 2
    hseq = a * hc[...] + u
    hc[...] = hseq[tm - 1:tm, :]
    o_ref[0] = (hseq * _gelu_tanh(gr_ref[0])).astype(o_ref.dtype)


def _lru(xr, gr, cw, cb, wg, ba, bx, lam, tm):
    b, s, w = xr.shape
    tok = pl.BlockSpec((1, tm, w), lambda i, j: (i, j, 0))
    full = lambda a: pl.BlockSpec(a.shape, lambda i, j: (0,) * a.ndim)
    return pl.pallas_call(
        functools.partial(_lru_kernel, tm=tm),
        grid=(b, s // tm),
        in_specs=[tok, tok, full(cw), full(cb), full(wg), full(ba), full(bx), full(lam)],
        out_specs=tok,
        out_shape=jax.ShapeDtypeStruct((b, s, w), BF16),
        scratch_shapes=[pltpu.VMEM((tm + 8, w), F32), pltpu.VMEM((1, w), F32)],
        compiler_params=_cparams(("parallel", "arbitrary")),
    )(xr, gr, cw, cb, wg, ba, bx, lam)


ROUTE_I1, ROUTE_I2, ROUTE_W1, ROUTE_W2, ROUTE_R1, ROUTE_R2 = range(6)


def _postmix_kernel(*refs, router):
    if router:
        (a_ref, r_ref, x_ref, gate_ref, og_ref, wo_ref, ng_ref, shift_ref, scale_ref, wr_ref, tri_ref,
         x_out, h_out, route_out, count_out, count_scr) = refs
    else:
        (a_ref, r_ref, x_ref, gate_ref, og_ref, wo_ref, ng_ref, shift_ref, scale_ref,
         x_out, h_out) = refs
    og = og_ref[...]
    ya = (_rms_rows(a_ref[0].astype(F32)) * og[:, :MLA_WIDTH]).astype(BF16)
    yr = (_rms_rows(r_ref[0].astype(F32)) * og[:, MLA_WIDTH:]).astype(BF16)
    y = (jnp.dot(ya, wo_ref[0:MLA_WIDTH, :], preferred_element_type=F32)
         + jnp.dot(yr, wo_ref[MLA_WIDTH:, :], preferred_element_type=F32))
    xn = x_ref[0] + gate_ref[0] * y
    x_out[0] = xn
    h2 = _rms_rows(xn) * ng_ref[...]
    h2 = h2 * (1.0 + scale_ref[0]) + shift_ref[0]
    h_out[0] = h2.astype(h_out.dtype)
    if router:
        @pl.when((pl.program_id(0) == 0) & (pl.program_id(1) == 0))
        def _():
            count_scr[...] = jnp.zeros_like(count_scr)

        logits = jnp.dot(h2, wr_ref[...], preferred_element_type=F32, precision=lax.Precision.HIGHEST)
        lane = lax.broadcasted_iota(jnp.int32, logits.shape, 1)
        lg = jnp.where(lane < N_EXPERTS, logits, -jnp.inf)
        m1 = jnp.max(lg, axis=-1, keepdims=True)
        i1 = jnp.min(jnp.where(lg == m1, lane, ROUTER_PAD), axis=-1, keepdims=True)
        lg2 = jnp.where(lane == i1, -jnp.inf, lg)
        m2 = jnp.max(lg2, axis=-1, keepdims=True)
        i2 = jnp.min(jnp.where(lg2 == m2, lane, ROUTER_PAD), axis=-1, keepdims=True)
        e2 = jnp.exp(m2 - m1)
        den = 1.0 + e2
        sel1 = lane == i1
        sel2 = lane == i2
        sel = jnp.where(sel1, 1.0, jnp.where(sel2, 1.0, 0.0))
        before = count_scr[...] + jnp.dot(tri_ref[...], sel.astype(BF16), preferred_element_type=F32)
        r1 = jnp.sum(jnp.where(sel1, before, 0.0), axis=-1, keepdims=True)
        r2 = jnp.sum(jnp.where(sel2, before, 0.0), axis=-1, keepdims=True)
        count_scr[...] += jnp.sum(sel, axis=0, keepdims=True)
        count_out[...] = count_scr[...]
        rec = jnp.zeros(logits.shape, F32)
        for ln, val in ((ROUTE_I1, i1.astype(F32)), (ROUTE_I2, i2.astype(F32)), (ROUTE_W1, 1.0 / den),
                        (ROUTE_W2, e2 / den), (ROUTE_R1, r1), (ROUTE_R2, r2)):
            rec = jnp.where(lane == ln, val, rec)
        route_out[0] = rec


def _postmix(attn, rec, x, gate, og, wo, ng, shift, scale, wr, tm):
    b, s, d = x.shape
    router = wr is not None
    tok = lambda w: pl.BlockSpec((1, tm, w), lambda i, j: (i, j, 0))
    per_b = pl.BlockSpec((1, 1, d), lambda i, j: (i, 0, 0))
    full = lambda a: pl.BlockSpec(a.shape, lambda i, j: (0,) * a.ndim)
    in_specs = [tok(MLA_WIDTH), tok(LRU_WIDTH), tok(d), per_b, full(og), full(wo), full(ng), per_b, per_b]
    args = [attn, rec, x, gate, og, wo, ng, shift, scale]
    out_specs = [tok(d), tok(d)]
    out_shape = [jax.ShapeDtypeStruct((b, s, d), F32), jax.ShapeDtypeStruct((b, s, d), F32 if router else BF16)]
    scratch = []
    if router:
        tri = jnp.asarray(np.tril(np.ones((tm, tm), np.float32), -1), dtype=BF16)
        in_specs += [full(wr), full(tri)]
        args += [wr, tri]
        out_specs += [tok(ROUTER_PAD), pl.BlockSpec((1, ROUTER_PAD), lambda i, j: (0, 0))]
        out_shape += [jax.ShapeDtypeStruct((b, s, ROUTER_PAD), F32), jax.ShapeDtypeStruct((1, ROUTER_PAD), F32)]
        scratch = [pltpu.VMEM((1, ROUTER_PAD), F32)]
    return pl.pallas_call(
        functools.partial(_postmix_kernel, router=router),
        grid=(b, s // tm),
        in_specs=in_specs, out_specs=out_specs, out_shape=out_shape, scratch_shapes=scratch,
        compiler_params=_cparams(("arbitrary", "arbitrary") if router else ("parallel", "parallel")),
    )(*args)


def _swiglu_chunk(h, wg, wu, wd):
    g = jnp.dot(h, wg, preferred_element_type=F32)
    u = jnp.dot(h, wu, preferred_element_type=F32)
    act = (g * jax.nn.sigmoid(g)) * u
    return jnp.dot(act.astype(BF16), wd, preferred_element_type=F32)


def _ffn_kernel(h_ref, wg_ref, wu_ref, wd_ref, x_ref, gate_ref, o_ref, acc, *, n_f):
    f = pl.program_id(1)

    @pl.when(f == 0)
    def _():
        acc[...] = jnp.zeros_like(acc)

    acc[...] += _swiglu_chunk(h_ref[...], wg_ref[...], wu_ref[...], wd_ref[...])

    @pl.when(f == n_f - 1)
    def _():
        o_ref[...] = x_ref[...] + gate_ref[0] * acc[...]


def _ffn(h, wg, wu, wd, x, gate, seq, tm, tf):
    n, d = x.shape
    n_f = wg.shape[1] // tf
    tiles_per_seq = seq // tm
    return pl.pallas_call(
        functools.partial(_ffn_kernel, n_f=n_f),
        grid=(n // tm, n_f),
        in_specs=[pl.BlockSpec((tm, d), lambda i, f: (i, 0)),
                  pl.BlockSpec((d, tf), lambda i, f: (0, f)),
                  pl.BlockSpec((d, tf), lambda i, f: (0, f)),
                  pl.BlockSpec((tf, d), lambda i, f: (f, 0)),
                  pl.BlockSpec((tm, d), lambda i, f: (i, 0)),
                  pl.BlockSpec((1, 1, d), lambda i, f: (i // tiles_per_seq, 0, 0))],
        out_specs=pl.BlockSpec((tm, d), lambda i, f: (i, 0)),
        out_shape=jax.ShapeDtypeStruct((n, d), F32),
        scratch_shapes=[pltpu.VMEM((tm, d), F32)],
        compiler_params=_cparams(("parallel", "arbitrary")),
    )(h, wg, wu, wd, x, gate)


ROW_UNROLL = 8


def _start_row_copies(tm, copy_of):
    def body(i, c):
        for u in range(ROW_UNROLL):
            for k in range(2):
                copy_of(i * ROW_UNROLL + u, k).start(priority=k)
        return c

    lax.fori_loop(0, tm // ROW_UNROLL, body, 0)


def _wait_bytes_of(ref, sem, times):
    for _ in range(times):
        pltpu.make_async_copy(ref, ref, sem).wait()


def _dispatch_kernel(pad_ref, slot_ref, h_ref, xs_ref, zbuf, sem, *, tm, tm_e):
    @pl.when(pl.program_id(0) == 0)
    def _():
        zbuf[...] = jnp.zeros_like(zbuf)
        for e in range(2 * N_EXPERTS):
            pad = pl.multiple_of(pad_ref[e], 8)
            zero_copy = pltpu.make_async_copy(zbuf, xs_ref.at[pl.ds(pad, tm_e), :], sem)
            zero_copy.start()
            zero_copy.wait()

    _start_row_copies(tm, lambda r, k: pltpu.make_async_copy(
        h_ref.at[pl.ds(r, 1), :], xs_ref.at[pl.ds(slot_ref[0, 0, 2 * r + k], 1), :], sem))
    _wait_bytes_of(h_ref, sem, 2)


def _dispatch(pad_start, slots, h, n_rows, tm, tm_e):
    n, d = h.shape
    grid_spec = pltpu.PrefetchScalarGridSpec(
        num_scalar_prefetch=1,
        grid=(n // tm,),
        in_specs=[pl.BlockSpec((1, 1, 2 * tm), lambda i, pad: (i, 0, 0), memory_space=pltpu.SMEM),
                  pl.BlockSpec((tm, d), lambda i, pad: (i, 0))],
        out_specs=pl.BlockSpec(memory_space=pl.ANY),
        scratch_shapes=[pltpu.VMEM((tm_e, d), F32), pltpu.SemaphoreType.DMA(())])
    return pl.pallas_call(
        functools.partial(_dispatch_kernel, tm=tm, tm_e=tm_e),
        grid_spec=grid_spec,
        out_shape=jax.ShapeDtypeStruct((n_rows, d), F32),
        compiler_params=_cparams(("arbitrary",)),
    )(pad_start, slots, h)


def _expert_kernel(te_ref, nu_ref, x_ref, wg_ref, wu_ref, wd_ref, o_ref, acc, *, n_f):
    del te_ref
    i = pl.program_id(0)
    f = pl.program_id(1)

    @pl.when(f == 0)
    def _():
        acc[...] = jnp.zeros_like(acc)

    @pl.when(i < nu_ref[0])
    def _():
        acc[...] += _swiglu_chunk(x_ref[...].astype(BF16), wg_ref[0, 0].astype(BF16),
                                  wu_ref[0, 0].astype(BF16), wd_ref[0, 0].astype(BF16))

    @pl.when(f == n_f - 1)
    def _():
        o_ref[...] = acc[...]


def _experts(tile_expert, n_used, xs, wg, wu, wd, layer, tm, tf):
    p, d = xs.shape
    n_f = wg.shape[3] // tf
    f_of = lambda i, f, nu: jnp.where(i < nu[0], f, n_f - 1)
    grid_spec = pltpu.PrefetchScalarGridSpec(
        num_scalar_prefetch=2,
        grid=(p // tm, n_f),
        in_specs=[pl.BlockSpec((tm, d), lambda i, f, te, nu: (jnp.minimum(i, nu[0] - 1), 0)),
                  pl.BlockSpec((1, 1, d, tf), lambda i, f, te, nu: (layer, te[i], 0, f_of(i, f, nu))),
                  pl.BlockSpec((1, 1, d, tf), lambda i, f, te, nu: (layer, te[i], 0, f_of(i, f, nu))),
                  pl.BlockSpec((1, 1, tf, d), lambda i, f, te, nu: (layer, te[i], f_of(i, f, nu), 0))],
        out_specs=pl.BlockSpec((tm, d), lambda i, f, te, nu: (i, 0)),
        scratch_shapes=[pltpu.VMEM((tm, d), F32)])
    return pl.pallas_call(
        functools.partial(_expert_kernel, n_f=n_f),
        grid_spec=grid_spec,
        out_shape=jax.ShapeDtypeStruct((p, d), F32),
        compiler_params=_cparams(("arbitrary", "arbitrary")),
    )(tile_expert, n_used, xs, wg, wu, wd)


def _combine_kernel(slot_ref, ys_ref, route_ref, x_ref, gate_ref, o_ref, buf, sem, *, tm):
    _start_row_copies(tm, lambda r, k: pltpu.make_async_copy(
        ys_ref.at[pl.ds(slot_ref[0, 0, 2 * r + k], 1), :], buf.at[k, pl.ds(r, 1), :], sem))
    _wait_bytes_of(buf.at[0], sem, 2)
    route = route_ref[...]
    w1 = route[:, ROUTE_W1:ROUTE_W1 + 1]
    w2 = route[:, ROUTE_W2:ROUTE_W2 + 1]
    o_ref[...] = x_ref[...] + gate_ref[0] * (w1 * buf[0] + w2 * buf[1])


def _combine(slots, ys, route, x, gate, seq, tm):
    n, d = x.shape
    tiles_per_seq = seq // tm
    return pl.pallas_call(
        functools.partial(_combine_kernel, tm=tm),
        grid=(n // tm,),
        in_specs=[pl.BlockSpec((1, 1, 2 * tm), lambda i: (i, 0, 0), memory_space=pltpu.SMEM),
                  pl.BlockSpec(memory_space=pl.ANY),
                  pl.BlockSpec((tm, ROUTER_PAD), lambda i: (i, 0)),
                  pl.BlockSpec((tm, d), lambda i: (i, 0)),
                  pl.BlockSpec((1, 1, d), lambda i: (i // tiles_per_seq, 0, 0))],
        out_specs=pl.BlockSpec((tm, d), lambda i: (i, 0)),
        out_shape=jax.ShapeDtypeStruct((n, d), F32),
        scratch_shapes=[pltpu.VMEM((2, tm, d), F32), pltpu.SemaphoreType.DMA(())],
        compiler_params=_cparams(("arbitrary",)),
    )(slots, ys, route, x, gate)


def _moe(h, route, counts, wg, wu, wd, layer, x, gate, seq, tm, tm_e, tf):
    n, d = x.shape
    cnt = counts[0, :N_EXPERTS].astype(jnp.int32)
    tiles = (cnt + tm_e - 1) // tm_e
    tile_end = jnp.cumsum(tiles)
    row_start = (tile_end - tiles) * tm_e
    n_tiles = 2 * n // tm_e + N_EXPERTS
    tile_expert = jnp.minimum(jnp.sum(jnp.arange(n_tiles)[:, None] >= tile_end[None, :], axis=1),
                              N_EXPERTS - 1).astype(jnp.int32)
    n_used = tile_end[-1:].astype(jnp.int32)
    expert = route[:, ROUTE_I1:ROUTE_I2 + 1].astype(jnp.int32)
    rank = route[:, ROUTE_R1:ROUTE_R2 + 1].astype(jnp.int32)
    slots = (row_start[expert] + rank).reshape(n // tm, 1, 2 * tm)
    n_rows = n_tiles * tm_e
    pad_start = jnp.minimum((row_start + cnt) // 8 * 8, n_rows - tm_e)
    tail_start = jnp.minimum(tile_end[-1] + jnp.arange(N_EXPERTS), n_tiles - 1) * tm_e
    pad_start = jnp.concatenate([pad_start, tail_start]).astype(jnp.int32)
    xs = _dispatch(pad_start, slots, h, n_rows, tm, tm_e)
    ys = _experts(tile_expert, n_used, xs, wg, wu, wd, layer, tm_e, tf)
    return _combine(slots, ys, route, x, gate, seq, tm)


def _group_mean_matrix():
    g = np.zeros((HEAD_PAD, HEAD_PAD), np.float32)
    g[:NOPE_DIM, :NOPE_DIM] = 1.0 / NOPE_DIM
    g[NOPE_DIM:QK_DIM, NOPE_DIM:QK_DIM] = 1.0 / ROPE_DIM
    return jnp.asarray(g, dtype=BF16)


def _pad_cols(a, left, total):
    return jnp.pad(a, ((0, 0), (left, total - left - a.shape[1])))


def _layer_weights(l, w_in, w_q_up, w_kv_up, q_gain, k_gain, w_a, w_x):
    wi = w_in[l]
    n_lat = Q_RANK + KV_RANK
    wz = jnp.concatenate([wi[:, :n_lat], _pad_cols(wi[:, n_lat:n_lat + ROPE_DIM], NOPE_DIM, HEAD_PAD),
                          wi[:, n_lat + ROPE_DIM:]], axis=1).astype(BF16)
    wq = jnp.pad(w_q_up[l].reshape(Q_RANK, MLA_HEADS, QK_DIM),
                 ((0, 0), (0, 0), (0, HEAD_PAD - QK_DIM))).reshape(Q_RANK, QK_PAD).astype(BF16)
    wkv = w_kv_up[l].reshape(KV_RANK, MLA_HEADS, NOPE_DIM + V_DIM)
    wk = jnp.pad(wkv[..., :NOPE_DIM], ((0, 0), (0, 0), (0, HEAD_PAD - NOPE_DIM)))
    wk = wk.reshape(KV_RANK, QK_PAD).astype(BF16)
    wv = jnp.pad(wkv[..., NOPE_DIM:], ((0, 0), (0, 0), (0, HEAD_PAD - V_DIM)))
    wv = wv.reshape(KV_RANK, QK_PAD).astype(BF16)
    qg = _pad_cols(q_gain[l][None, :], 0, HEAD_PAD)
    kg = _pad_cols(k_gain[l][None, :NOPE_DIM], 0, HEAD_PAD)
    krg = _pad_cols(k_gain[l][None, NOPE_DIM:], NOPE_DIM, HEAD_PAD)
    eye = jnp.eye(LRU_BLOCKS, dtype=F32)
    blockdiag = lambda w: jnp.einsum('gij,gh->gihj', w, eye).reshape(LRU_WIDTH, LRU_WIDTH)
    wgate = jnp.concatenate([blockdiag(w_a[l]), blockdiag(w_x[l])], axis=1).astype(BF16)
    return wz, wq, wk, wv, qg, kg, krg, wgate


def kernel(x, c, positions, w_ada, b_ada, norm_gain, w_in, latent_gain, w_q_up, w_kv_up, q_gain, k_gain,
           conv_w, conv_b, w_a, b_a, w_x, b_x, lru_lambda, out_gain, w_out, ffn_w_gate, ffn_w_up,
           ffn_w_down, router_w, moe_w_gate, moe_w_up, moe_w_down):
    b, s, d = x.shape
    n = b * s
    tm = min(512, s)
    tq = min(512, s)
    tk = min(512, s)

    c_pad = jnp.pad(c, ((0, 8 - b), (0, 0)))
    mods = _ada_all(c_pad, w_ada.reshape(DEPTH * 2, d, 3 * d), b_ada.reshape(DEPTH * 2, 1, 3 * d))
    mods = mods[:, :b].reshape(DEPTH, 2, b, 1, 3, d)
    shift_of = lambda l, i: mods[l, i, :, :, 0]
    scale_of = lambda l, i: mods[l, i, :, :, 1]
    gate_of = lambda l, i: mods[l, i, :, :, 2]

    inv_freq = ROPE_THETA ** (-jnp.arange(0, ROPE_DIM, 2, dtype=F32) / ROPE_DIM)
    invf = _pad_cols(jnp.concatenate([inv_freq, inv_freq])[None, :], NOPE_DIM, HEAD_PAD)
    pos_col = positions.reshape(b, s, 1)
    pos_row = positions.reshape(b, 1, s)
    cos, s1, s2 = _rope_tables(pos_col, invf, tm)
    gm = _group_mean_matrix()

    for l in range(DEPTH):
        wz, wq, wk, wv, qg, kg, krg, wgate = _layer_weights(l, w_in, w_q_up, w_kv_up, q_gain, k_gain, w_a, w_x)
        q, k, v, xr, gr = _premix(x, cos, s1, s2, shift_of(l, 0), scale_of(l, 0), norm_gain[l, 0][None, :],
                                  wz, latent_gain[l][None, :], wq, wk, wv, gm, qg, kg, krg, tm)
        attn = _attention(q, k, v, pos_col, pos_row, tq, tk)
        rec = _lru(xr, gr, conv_w[l], conv_b[l][None, :], wgate, b_a[l][None, :], b_x[l][None, :],
                   lru_lambda[l][None, :], tm)
        j = l // 2
        moe = l % 2 == 1
        wr = jnp.pad(router_w[j], ((0, 0), (0, ROUTER_PAD - N_EXPERTS))) if moe else None
        outs = _postmix(attn, rec, x, gate_of(l, 0), out_gain[l][None, :], w_out[l].astype(BF16),
                        norm_gain[l, 1][None, :], shift_of(l, 1), scale_of(l, 1), wr, tm)
        x_mid, h2 = outs[0].reshape(n, d), outs[1].reshape(n, d)
        if moe:
            x = _moe(h2, outs[2].reshape(n, ROUTER_PAD), outs[3], moe_w_gate, moe_w_up, moe_w_down, j,
                     x_mid, gate_of(l, 1), s, tm, min(1024, n), 512)
        else:
            x = _ffn(h2, ffn_w_gate[j].astype(BF16), ffn_w_up[j].astype(BF16), ffn_w_down[j].astype(BF16),
                     x_mid, gate_of(l, 1), s, tm, 1408)
        x = x.reshape(b, s, d)
    return x
```

```python
import functools

import numpy as np
import jax
import jax.numpy as jnp
from jax import lax
from jax.experimental import pallas as pl
from jax.experimental.pallas import tpu as pltpu

F32 = jnp.float32
BF16 = jnp.bfloat16

D_MODEL = 1024
DEPTH = 4
MLA_HEADS = 8
NOPE_DIM = 64
ROPE_DIM = 32
HALF_ROPE = ROPE_DIM // 2
V_DIM = 64
QK_DIM = NOPE_DIM + ROPE_DIM
Q_RANK = 256
KV_RANK = 128
MLA_WIDTH = MLA_HEADS * V_DIM
LRU_WIDTH = D_MODEL - MLA_WIDTH
LRU_BLOCKS = 8
LRU_BLOCK = LRU_WIDTH // LRU_BLOCKS
CONV_WIDTH = 4
LRU_C = 8.0
N_EXPERTS = 8
ROPE_THETA = 10000.0
EPS = 1e-6

HEAD_PAD = 128
QK_PAD = MLA_HEADS * HEAD_PAD
Z_COLS = Q_RANK + KV_RANK + HEAD_PAD + 2 * LRU_WIDTH
ROUTER_PAD = 128
SM_SCALE = QK_DIM ** -0.5 * float(np.log2(np.e))
NEG = float(np.finfo(np.float32).min)
VMEM_LIMIT = 56 * 1024 * 1024


def _cparams(sem):
    return pltpu.CompilerParams(dimension_semantics=sem, vmem_limit_bytes=VMEM_LIMIT)


def _rms_rows(v):
    return v * lax.rsqrt(jnp.mean(v * v, axis=-1, keepdims=True) + EPS)


def _ada_kernel(c_ref, w_ref, b_ref, o_ref):
    c = c_ref[...]
    c_act = c * jax.nn.sigmoid(c)
    o_ref[0] = jnp.dot(c_act, w_ref[0], preferred_element_type=F32,
                       precision=lax.Precision.HIGHEST) + b_ref[0]


def _ada_all(c_pad, w_ada, b_ada):
    l2 = w_ada.shape[0]
    tn = 1536
    return pl.pallas_call(
        _ada_kernel,
        grid=(l2, 3 * D_MODEL // tn),
        in_specs=[pl.BlockSpec((8, D_MODEL), lambda l, n: (0, 0)),
                  pl.BlockSpec((1, D_MODEL, tn), lambda l, n: (l, 0, n)),
                  pl.BlockSpec((1, 1, tn), lambda l, n: (l, 0, n))],
        out_specs=pl.BlockSpec((1, 8, tn), lambda l, n: (l, 0, n)),
        out_shape=jax.ShapeDtypeStruct((l2, 8, 3 * D_MODEL), F32),
        compiler_params=_cparams(("parallel", "parallel")),
    )(c_pad, w_ada, b_ada)


def _rope_kernel(pos_ref, invf_ref, cos_ref, s1_ref, s2_ref):
    ang = pos_ref[0].astype(F32) * invf_ref[...]
    c = jnp.cos(ang)
    s = jnp.sin(ang)
    lane = lax.broadcasted_iota(jnp.int32, ang.shape, 1)
    first = (lane >= NOPE_DIM) & (lane < NOPE_DIM + HALF_ROPE)
    second = (lane >= NOPE_DIM + HALF_ROPE) & (lane < QK_DIM)
    cos_ref[0] = c
    s1_ref[0] = jnp.where(first, -s, 0.0)
    s2_ref[0] = jnp.where(second, s, 0.0)


def _rope_tables(pos3, invf, tm):
    b, s, _ = pos3.shape
    spec = pl.BlockSpec((1, tm, HEAD_PAD), lambda i, j: (i, j, 0))
    shp = jax.ShapeDtypeStruct((b, s, HEAD_PAD), F32)
    return pl.pallas_call(
        _rope_kernel,
        grid=(b, s // tm),
        in_specs=[pl.BlockSpec((1, tm, 1), lambda i, j: (i, j, 0)),
                  pl.BlockSpec((1, HEAD_PAD), lambda i, j: (0, 0))],
        out_specs=[spec, spec, spec],
        out_shape=[shp, shp, shp],
        compiler_params=_cparams(("parallel", "parallel")),
    )(pos3, invf)


def _premix_kernel(x_ref, cos_ref, s1_ref, s2_ref, shift_ref, scale_ref, ng_ref, wz_ref, lg_ref,
                   wq_ref, wk_ref, wv_ref, gm_ref, qg_ref, kg_ref, krg_ref,
                   q_out, k_out, v_out, xr_out, gr_out):
    x = x_ref[0]
    h = _rms_rows(x) * ng_ref[...]
    h = h * (1.0 + scale_ref[0]) + shift_ref[0]
    z = jnp.dot(h.astype(BF16), wz_ref[...], preferred_element_type=F32)
    xr_out[0] = z[:, Q_RANK + KV_RANK + HEAD_PAD:Q_RANK + KV_RANK + HEAD_PAD + LRU_WIDTH]
    gr_out[0] = z[:, Q_RANK + KV_RANK + HEAD_PAD + LRU_WIDTH:]

    lg = lg_ref[...]
    cq = _rms_rows(z[:, :Q_RANK]) * lg[:, :Q_RANK]
    ckv = (_rms_rows(z[:, Q_RANK:Q_RANK + KV_RANK]) * lg[:, Q_RANK:]).astype(BF16)
    kr = z[:, Q_RANK + KV_RANK:Q_RANK + KV_RANK + HEAD_PAD]

    qf = jnp.dot(cq.astype(BF16), wq_ref[...], preferred_element_type=F32)
    kf = jnp.dot(ckv, wk_ref[...], preferred_element_type=F32)
    vf = jnp.dot(ckv, wv_ref[...], preferred_element_type=F32)
    vlane = lax.broadcasted_iota(jnp.int32, vf.shape, 1)
    v_out[0] = jnp.where(vlane % HEAD_PAD == V_DIM, 1.0, vf).astype(BF16)

    cos = cos_ref[0]
    s1 = s1_ref[0]
    s2 = s2_ref[0]
    gm = gm_ref[...]

    def group_norm(blk):
        msq = jnp.dot((blk * blk).astype(BF16), gm, preferred_element_type=F32)
        return blk * lax.rsqrt(msq + EPS)

    def rope(blk):
        return (blk * cos + pltpu.roll(blk, HEAD_PAD - HALF_ROPE, 1) * s1
                + pltpu.roll(blk, HALF_ROPE, 1) * s2)

    kr_rot = rope(group_norm(kr) * krg_ref[...])
    qg = qg_ref[...]
    kg = kg_ref[...]
    for hd in range(MLA_HEADS):
        sl = slice(hd * HEAD_PAD, (hd + 1) * HEAD_PAD)
        qh = rope(group_norm(qf[:, sl]) * qg) * SM_SCALE
        q_out[0, :, sl] = qh.astype(BF16)
        kh = group_norm(kf[:, sl]) * kg + kr_rot
        k_out[0, :, sl] = kh.astype(BF16)


def _premix(x, cos, s1, s2, shift, scale, ng, wz, lgain, wq, wk, wv, gm, qg, kg, krg, tm):
    b, s, d = x.shape
    tok = lambda w: pl.BlockSpec((1, tm, w), lambda i, j: (i, j, 0))
    per_b = pl.BlockSpec((1, 1, d), lambda i, j: (i, 0, 0))
    full = lambda a: pl.BlockSpec(a.shape, lambda i, j: (0,) * a.ndim)
    return pl.pallas_call(
        _premix_kernel,
        grid=(b, s // tm),
        in_specs=[tok(d), tok(HEAD_PAD), tok(HEAD_PAD), tok(HEAD_PAD), per_b, per_b, full(ng), full(wz),
                  full(lgain), full(wq), full(wk), full(wv), full(gm), full(qg), full(kg), full(krg)],
        out_specs=[tok(QK_PAD), tok(QK_PAD), tok(QK_PAD), tok(LRU_WIDTH), tok(LRU_WIDTH)],
        out_shape=[jax.ShapeDtypeStruct((b, s, QK_PAD), BF16),
                   jax.ShapeDtypeStruct((b, s, QK_PAD), BF16),
                   jax.ShapeDtypeStruct((b, s, QK_PAD), BF16),
                   jax.ShapeDtypeStruct((b, s, LRU_WIDTH), F32),
                   jax.ShapeDtypeStruct((b, s, LRU_WIDTH), F32)],
        compiler_params=_cparams(("parallel", "parallel")),
    )(x, cos, s1, s2, shift, scale, ng, wz, lgain, wq, wk, wv, gm, qg, kg, krg)


SUBLANES = 8
ATTN_ROWS = 32
ATTN_HEADS = 4
LANES = 128


def _attn_kernel(q_ref, k_ref, v_ref, pq_ref, pk_ref, o_ref, s_scr, p_scr, m_scr, acc_scr, pq_scr, *, tq, tk):
    assert tq == tk
    qi = pl.program_id(2)
    m_scr[...] = jnp.full(m_scr.shape, NEG, F32)
    acc_scr[...] = jnp.zeros(acc_scr.shape, F32)
    pq_scr[...] = jnp.broadcast_to(pq_ref[0], pq_scr.shape)
    head_lanes = [slice(hh * HEAD_PAD, (hh + 1) * HEAD_PAD) for hh in range(ATTN_HEADS)]

    def chunk(j, width, masked):
        ks = pl.multiple_of(j * tk, tk)
        for hh, sl in enumerate(head_lanes):
            s_scr[hh, :, 0:width] = lax.dot_general(q_ref[0, :, sl], k_ref[0, pl.ds(ks, width), sl],
                                                    (((1,), (1,)), ((), ())), preferred_element_type=F32)
        for hh, sl in enumerate(head_lanes):
            for rb in range(tq // ATTN_ROWS):
                rows = slice(rb * ATTN_ROWS, (rb + 1) * ATTN_ROWS)

                def block(cb):
                    blk = s_scr[hh, rows, cb * LANES:(cb + 1) * LANES]
                    if masked:
                        keep = pk_ref[0, :, pl.ds(ks + cb * LANES, LANES)] <= pq_scr[rows, :]
                        blk = jnp.where(keep, blk, NEG)
                    return blk

                rmax = block(0)
                for cb in range(1, width // LANES):
                    rmax = jnp.maximum(rmax, block(cb))
                m_old = m_scr[hh, rows, :]
                m_new = jnp.maximum(m_old, jnp.max(rmax, axis=-1, keepdims=True))
                m_scr[hh, rows, :] = m_new
                acc_scr[hh, rows, :] = acc_scr[hh, rows, :] * jnp.exp2(m_old - m_new)
                for cb in range(width // LANES):
                    p_scr[hh, rows, cb * LANES:(cb + 1) * LANES] = jnp.exp2(block(cb) - m_new).astype(BF16)
            acc_scr[hh] += jnp.dot(p_scr[hh, :, 0:width], v_ref[0, pl.ds(ks, width), sl],
                                   preferred_element_type=F32)

    def two_chunks(i, carry):
        chunk(2 * i, 2 * tk, False)
        return carry

    lax.fori_loop(0, qi // 2, two_chunks, 0)

    @pl.when(qi % 2 == 1)
    def _():
        chunk(qi - 1, tk, False)

    chunk(qi, tk, True)

    lane = lax.broadcasted_iota(jnp.int32, (tq, LANES), 1)
    for pair in range(ATTN_HEADS // 2):
        outs = []
        for hh in (2 * pair, 2 * pair + 1):
            acc = acc_scr[hh]
            den = jnp.sum(jnp.where(lane == V_DIM, acc, 0.0), axis=-1, keepdims=True)
            outs.append(acc / den)
        o_ref[0, :, pair * LANES:(pair + 1) * LANES] = jnp.where(
            lane < V_DIM, outs[0], pltpu.roll(outs[1], V_DIM, 1)).astype(o_ref.dtype)


def _attention(q, k, v, pos_col, pos_row, tq, tk):
    b, s, _ = q.shape
    groups = MLA_HEADS // ATTN_HEADS
    width = ATTN_HEADS * HEAD_PAD
    resident = lambda: pl.BlockSpec((1, s, width), lambda i, g, j: (i, 0, g), pipeline_mode=pl.Buffered(1))
    return pl.pallas_call(
        functools.partial(_attn_kernel, tq=tq, tk=tk),
        grid=(b, groups, s // tq),
        in_specs=[pl.BlockSpec((1, tq, width), lambda i, g, j: (i, j, g)),
                  resident(), resident(),
                  pl.BlockSpec((1, tq, 1), lambda i, g, j: (i, j, 0)),
                  pl.BlockSpec((1, 1, s), lambda i, g, j: (i, 0, 0))],
        out_specs=pl.BlockSpec((1, tq, ATTN_HEADS * V_DIM), lambda i, g, j: (i, j, g)),
        out_shape=jax.ShapeDtypeStruct((b, s, MLA_WIDTH), BF16),
        scratch_shapes=[pltpu.VMEM((ATTN_HEADS, tq, 2 * tk), F32), pltpu.VMEM((ATTN_HEADS, tq, 2 * tk), BF16),
                        pltpu.VMEM((ATTN_HEADS, tq, LANES), F32), pltpu.VMEM((ATTN_HEADS, tq, LANES), F32),
                        pltpu.VMEM((tq, LANES), jnp.int32)],
        compiler_params=_cparams(("parallel", "parallel", "arbitrary")),
    )(q, k, v, pos_col, pos_row)


def _gelu_tanh(x):
    return 0.5 * x * (1.0 + jnp.tanh(np.sqrt(2.0 / np.pi).astype(np.float32) * (x + 0.044715 * (x * x * x))))


def _lru_tile(xr, gr, cw_ref, cb_ref, wg_ref, ba_ref, bx_ref, lam_ref, xbuf, hc, ti, tm):
    halo = SUBLANES

    @pl.when(ti == 0)
    def _():
        xbuf[0:halo, :] = jnp.zeros((halo, LRU_WIDTH), F32)
        hc[...] = jnp.zeros_like(hc)

    @pl.when(ti > 0)
    def _():
        xbuf[0:halo, :] = xbuf[tm:tm + halo, :]

    xbuf[halo:tm + halo, :] = xr
    cw = cw_ref[...]
    xc = cb_ref[...] + cw[CONV_WIDTH - 1:CONV_WIDTH, :] * xbuf[halo:tm + halo, :]
    for tap in range(1, CONV_WIDTH):
        xc = xc + cw[CONV_WIDTH - 1 - tap:CONV_WIDTH - tap, :] * xbuf[halo - tap:tm + halo - tap, :]

    gz = jnp.dot(xc.astype(BF16), wg_ref[...], preferred_element_type=F32)
    r = jax.nn.sigmoid(gz[:, :LRU_WIDTH] + ba_ref[...])
    gi = jax.nn.sigmoid(gz[:, LRU_WIDTH:] + bx_ref[...])
    lam = lam_ref[...]
    log_sig = -(jnp.maximum(-lam, 0.0) + jnp.log1p(jnp.exp(-jnp.abs(lam))))
    log_a = LRU_C * r * log_sig
    a = jnp.exp(log_a)
    u = jnp.sqrt(-jnp.tanh(log_a) * (a * a + 1.0)) * (gi * xc)

    sub = lax.broadcasted_iota(jnp.int32, (tm, LRU_WIDTH), 0) % SUBLANES
    d = 1
    while d < SUBLANES:
        keep = sub >= d
        a_prev = jnp.where(keep, pltpu.roll(a, d, 0), 1.0)
        u_prev = jnp.where(keep, pltpu.roll(u, d, 0), 0.0)
        u = a * u_prev + u
        a = a * a_prev
        d *= 2
    h_in = hc[...]
    groups = []
    for g in range(tm // SUBLANES):
        rows = slice(g * SUBLANES, (g + 1) * SUBLANES)
        blk = a[rows, :] * h_in + u[rows, :]
        groups.append(blk)
        h_in = blk[SUBLANES - 1:SUBLANES, :]
    hc[...] = h_in
    hseq = jnp.concatenate(groups, axis=0)
    return hseq * _gelu_tanh(gr)


def _lru_kernel(xr_ref, gr_ref, cw_ref, cb_ref, wg_ref, ba_ref, bx_ref, lam_ref, o_ref, xbuf, hc, *, tm):
    o_ref[0] = _lru_tile(xr_ref[0], gr_ref[0], cw_ref, cb_ref, wg_ref, ba_ref, bx_ref, lam_ref, xbuf, hc,
                         pl.program_id(1), tm).astype(o_ref.dtype)


def _lru(xr, gr, cw, cb, wg, ba, bx, lam, tm):
    b, s, w = xr.shape
    tok = pl.BlockSpec((1, tm, w), lambda i, j: (i, j, 0))
    full = lambda a: pl.BlockSpec(a.shape, lambda i, j: (0,) * a.ndim)
    return pl.pallas_call(
        functools.partial(_lru_kernel, tm=tm),
        grid=(b, s // tm),
        in_specs=[tok, tok, full(cw), full(cb), full(wg), full(ba), full(bx), full(lam)],
        out_specs=tok,
        out_shape=jax.ShapeDtypeStruct((b, s, w), BF16),
        scratch_shapes=[pltpu.VMEM((tm + SUBLANES, w), F32), pltpu.VMEM((1, w), F32)],
        compiler_params=_cparams(("parallel", "arbitrary")),
    )(xr, gr, cw, cb, wg, ba, bx, lam)


ROUTE_I1, ROUTE_I2, ROUTE_W1, ROUTE_W2, ROUTE_R1, ROUTE_R2 = range(6)


def _postmix_kernel(*refs, router):
    if router:
        (a_ref, r_ref, x_ref, gate_ref, og_ref, wo_ref, ng_ref, shift_ref, scale_ref, wr_ref, tri_ref,
         x_out, h_out, route_out, count_out, count_scr) = refs
    else:
        (a_ref, r_ref, x_ref, gate_ref, og_ref, wo_ref, ng_ref, shift_ref, scale_ref,
         x_out, h_out) = refs
    og = og_ref[...]
    ya = (_rms_rows(a_ref[0].astype(F32)) * og[:, :MLA_WIDTH]).astype(BF16)
    yr = (_rms_rows(r_ref[0].astype(F32)) * og[:, MLA_WIDTH:]).astype(BF16)
    y = (jnp.dot(ya, wo_ref[0:MLA_WIDTH, :], preferred_element_type=F32)
         + jnp.dot(yr, wo_ref[MLA_WIDTH:, :], preferred_element_type=F32))
    xn = x_ref[0] + gate_ref[0] * y
    x_out[0] = xn
    h2 = _rms_rows(xn) * ng_ref[...]
    h2 = h2 * (1.0 + scale_ref[0]) + shift_ref[0]
    h_out[0] = h2.astype(h_out.dtype)
    if router:
        @pl.when((pl.program_id(0) == 0) & (pl.program_id(1) == 0))
        def _():
            count_scr[...] = jnp.zeros_like(count_scr)

        h_hi = h2.astype(BF16)
        h_lo = (h2 - h_hi.astype(F32)).astype(BF16)
        wr = wr_ref[...]
        w_hi = wr.astype(BF16)
        w_lo = (wr - w_hi.astype(F32)).astype(BF16)
        logits = (jnp.dot(h_hi, w_hi, preferred_element_type=F32)
                  + (jnp.dot(h_hi, w_lo, preferred_element_type=F32)
                     + jnp.dot(h_lo, w_hi, preferred_element_type=F32)))
        lane = lax.broadcasted_iota(jnp.int32, logits.shape, 1)
        lg = jnp.where(lane < N_EXPERTS, logits, -jnp.inf)
        m1 = jnp.max(lg, axis=-1, keepdims=True)
        i1 = jnp.min(jnp.where(lg == m1, lane, ROUTER_PAD), axis=-1, keepdims=True)
        lg2 = jnp.where(lane == i1, -jnp.inf, lg)
        m2 = jnp.max(lg2, axis=-1, keepdims=True)
        i2 = jnp.min(jnp.where(lg2 == m2, lane, ROUTER_PAD), axis=-1, keepdims=True)
        e2 = jnp.exp(m2 - m1)
        den = 1.0 + e2
        sel1 = lane == i1
        sel2 = lane == i2
        sel = jnp.where(sel1, 1.0, jnp.where(sel2, 1.0, 0.0))
        before = count_scr[...] + jnp.dot(tri_ref[...], sel.astype(BF16), preferred_element_type=F32)
        r1 = jnp.sum(jnp.where(sel1, before, 0.0), axis=-1, keepdims=True)
        r2 = jnp.sum(jnp.where(sel2, before, 0.0), axis=-1, keepdims=True)
        count_scr[...] += jnp.sum(sel, axis=0, keepdims=True)
        count_out[...] = count_scr[...]
        rec = jnp.zeros(logits.shape, F32)
        for ln, val in ((ROUTE_I1, i1.astype(F32)), (ROUTE_I2, i2.astype(F32)), (ROUTE_W1, 1.0 / den),
                        (ROUTE_W2, e2 / den), (ROUTE_R1, r1), (ROUTE_R2, r2)):
            rec = jnp.where(lane == ln, val, rec)
        route_out[0] = rec


def _postmix(attn, rec, x, gate, og, wo, ng, shift, scale, wr, tm):
    b, s, d = x.shape
    router = wr is not None
    tok = lambda w: pl.BlockSpec((1, tm, w), lambda i, j: (i, j, 0))
    per_b = pl.BlockSpec((1, 1, d), lambda i, j: (i, 0, 0))
    full = lambda a: pl.BlockSpec(a.shape, lambda i, j: (0,) * a.ndim)
    in_specs = [tok(MLA_WIDTH), tok(LRU_WIDTH), tok(d), per_b, full(og), full(wo), full(ng), per_b, per_b]
    args = [attn, rec, x, gate, og, wo, ng, shift, scale]
    out_specs = [tok(d), tok(d)]
    out_shape = [jax.ShapeDtypeStruct((b, s, d), F32), jax.ShapeDtypeStruct((b, s, d), F32 if router else BF16)]
    scratch = []
    if router:
        tri = jnp.asarray(np.tril(np.ones((tm, tm), np.float32), -1), dtype=BF16)
        in_specs += [full(wr), full(tri)]
        args += [wr, tri]
        out_specs += [tok(ROUTER_PAD), pl.BlockSpec((1, ROUTER_PAD), lambda i, j: (0, 0))]
        out_shape += [jax.ShapeDtypeStruct((b, s, ROUTER_PAD), F32), jax.ShapeDtypeStruct((1, ROUTER_PAD), F32)]
        scratch = [pltpu.VMEM((1, ROUTER_PAD), F32)]
    return pl.pallas_call(
        functools.partial(_postmix_kernel, router=router),
        grid=(b, s // tm),
        in_specs=in_specs, out_specs=out_specs, out_shape=out_shape, scratch_shapes=scratch,
        compiler_params=_cparams(("arbitrary", "arbitrary") if router else ("parallel", "parallel")),
    )(*args)


FF_SUB = 256


def _swiglu_chunk(h, wg_ref, wu_ref, wd_ref):
    out = None
    for c in range(wg_ref.shape[1] // FF_SUB):
        cols = slice(c * FF_SUB, (c + 1) * FF_SUB)
        g = jnp.dot(h, wg_ref[:, cols].astype(BF16), preferred_element_type=F32)
        u = jnp.dot(h, wu_ref[:, cols].astype(BF16), preferred_element_type=F32)
        act = (g * jax.nn.sigmoid(g)) * u
        part = jnp.dot(act.astype(BF16), wd_ref[cols, :].astype(BF16), preferred_element_type=F32)
        out = part if out is None else out + part
    return out


def _ffn_kernel(h_ref, wg_ref, wu_ref, wd_ref, x_ref, gate_ref, o_ref):
    o_ref[...] = x_ref[...] + gate_ref[0] * _swiglu_chunk(h_ref[...], wg_ref, wu_ref, wd_ref)


def _ffn(h, wg, wu, wd, x, gate, seq, tm):
    n, d = x.shape
    tiles_per_seq = seq // tm
    resident = lambda a: pl.BlockSpec(a.shape, lambda i: (0, 0), pipeline_mode=pl.Buffered(1))
    return pl.pallas_call(
        _ffn_kernel,
        grid=(n // tm,),
        in_specs=[pl.BlockSpec((tm, d), lambda i: (i, 0)),
                  resident(wg), resident(wu), resident(wd),
                  pl.BlockSpec((tm, d), lambda i: (i, 0)),
                  pl.BlockSpec((1, 1, d), lambda i: (i // tiles_per_seq, 0, 0))],
        out_specs=pl.BlockSpec((tm, d), lambda i: (i, 0)),
        out_shape=jax.ShapeDtypeStruct((n, d), F32),
        compiler_params=_cparams(("parallel",)),
    )(h, wg, wu, wd, x, gate)


ROW_UNROLL = 8


def _start_row_copies(tm, copy_of):
    def body(i, c):
        for u in range(ROW_UNROLL):
            for k in range(2):
                copy_of(i * ROW_UNROLL + u, k).start(priority=k)
        return c

    lax.fori_loop(0, tm // ROW_UNROLL, body, 0)


def _wait_bytes_of(ref, sem, times):
    for _ in range(times):
        pltpu.make_async_copy(ref, ref, sem).wait()


def _dispatch_kernel(pad_ref, slot_ref, h_ref, xs_ref, zbuf, sem, *, tm, tm_e):
    @pl.when(pl.program_id(0) == 0)
    def _():
        zbuf[...] = jnp.zeros_like(zbuf)
        for e in range(2 * N_EXPERTS):
            pad = pl.multiple_of(pad_ref[e], 8)
            zero_copy = pltpu.make_async_copy(zbuf, xs_ref.at[pl.ds(pad, tm_e), :], sem)
            zero_copy.start()
            zero_copy.wait()

    _start_row_copies(tm, lambda r, k: pltpu.make_async_copy(
        h_ref.at[pl.ds(r, 1), :], xs_ref.at[pl.ds(slot_ref[0, 0, 2 * r + k], 1), :], sem))
    _wait_bytes_of(h_ref, sem, 2)


def _dispatch(pad_start, slots, h, n_rows, tm, tm_e):
    n, d = h.shape
    grid_spec = pltpu.PrefetchScalarGridSpec(
        num_scalar_prefetch=1,
        grid=(n // tm,),
        in_specs=[pl.BlockSpec((1, 1, 2 * tm), lambda i, pad: (i, 0, 0), memory_space=pltpu.SMEM),
                  pl.BlockSpec((tm, d), lambda i, pad: (i, 0))],
        out_specs=pl.BlockSpec(memory_space=pl.ANY),
        scratch_shapes=[pltpu.VMEM((tm_e, d), F32), pltpu.SemaphoreType.DMA(())])
    return pl.pallas_call(
        functools.partial(_dispatch_kernel, tm=tm, tm_e=tm_e),
        grid_spec=grid_spec,
        out_shape=jax.ShapeDtypeStruct((n_rows, d), F32),
        compiler_params=_cparams(("arbitrary",)),
    )(pad_start, slots, h)


def _expert_kernel(te_ref, nu_ref, x_ref, wg_ref, wu_ref, wd_ref, o_ref, acc, *, n_f):
    del te_ref
    i = pl.program_id(0)
    f = pl.program_id(1)

    @pl.when(f == 0)
    def _():
        acc[...] = jnp.zeros_like(acc)

    @pl.when(i < nu_ref[0])
    def _():
        acc[...] += _swiglu_chunk(x_ref[...].astype(BF16), wg_ref.at[0, 0], wu_ref.at[0, 0], wd_ref.at[0, 0])

    @pl.when(f == n_f - 1)
    def _():
        o_ref[...] = acc[...]


def _experts(tile_expert, n_used, xs, wg, wu, wd, layer, tm, tf):
    p, d = xs.shape
    n_f = wg.shape[3] // tf
    f_of = lambda i, f, nu: jnp.where(i < nu[0], f, n_f - 1)
    grid_spec = pltpu.PrefetchScalarGridSpec(
        num_scalar_prefetch=2,
        grid=(p // tm, n_f),
        in_specs=[pl.BlockSpec((tm, d), lambda i, f, te, nu: (jnp.minimum(i, nu[0] - 1), 0)),
                  pl.BlockSpec((1, 1, d, tf), lambda i, f, te, nu: (layer, te[i], 0, f_of(i, f, nu))),
                  pl.BlockSpec((1, 1, d, tf), lambda i, f, te, nu: (layer, te[i], 0, f_of(i, f, nu))),
                  pl.BlockSpec((1, 1, tf, d), lambda i, f, te, nu: (layer, te[i], f_of(i, f, nu), 0))],
        out_specs=pl.BlockSpec((tm, d), lambda i, f, te, nu: (i, 0)),
        scratch_shapes=[pltpu.VMEM((tm, d), F32)])
    return pl.pallas_call(
        functools.partial(_expert_kernel, n_f=n_f),
        grid_spec=grid_spec,
        out_shape=jax.ShapeDtypeStruct((p, d), F32),
        compiler_params=_cparams(("arbitrary", "arbitrary")),
    )(tile_expert, n_used, xs, wg, wu, wd)


def _combine_kernel(slot_ref, ys_ref, route_ref, x_ref, gate_ref, o_ref, buf, sem, *, tm):
    _start_row_copies(tm, lambda r, k: pltpu.make_async_copy(
        ys_ref.at[pl.ds(slot_ref[0, 0, 2 * r + k], 1), :], buf.at[k, pl.ds(r, 1), :], sem))
    _wait_bytes_of(buf.at[0], sem, 2)
    route = route_ref[...]
    w1 = route[:, ROUTE_W1:ROUTE_W1 + 1]
    w2 = route[:, ROUTE_W2:ROUTE_W2 + 1]
    o_ref[...] = x_ref[...] + gate_ref[0] * (w1 * buf[0] + w2 * buf[1])


def _combine(slots, ys, route, x, gate, seq, tm):
    n, d = x.shape
    tiles_per_seq = seq // tm
    return pl.pallas_call(
        functools.partial(_combine_kernel, tm=tm),
        grid=(n // tm,),
        in_specs=[pl.BlockSpec((1, 1, 2 * tm), lambda i: (i, 0, 0), memory_space=pltpu.SMEM),
                  pl.BlockSpec(memory_space=pl.ANY),
                  pl.BlockSpec((tm, ROUTER_PAD), lambda i: (i, 0)),
                  pl.BlockSpec((tm, d), lambda i: (i, 0)),
                  pl.BlockSpec((1, 1, d), lambda i: (i // tiles_per_seq, 0, 0))],
        out_specs=pl.BlockSpec((tm, d), lambda i: (i, 0)),
        out_shape=jax.ShapeDtypeStruct((n, d), F32),
        scratch_shapes=[pltpu.VMEM((2, tm, d), F32), pltpu.SemaphoreType.DMA(())],
        compiler_params=_cparams(("arbitrary",)),
    )(slots, ys, route, x, gate)


def _moe(h, route, counts, wg, wu, wd, layer, x, gate, seq, tm, tm_e, tf):
    n, d = x.shape
    cnt = counts[0, :N_EXPERTS].astype(jnp.int32)
    tiles = (cnt + tm_e - 1) // tm_e
    tile_end = jnp.cumsum(tiles)
    row_start = (tile_end - tiles) * tm_e
    n_tiles = 2 * n // tm_e + N_EXPERTS
    tile_expert = jnp.minimum(jnp.sum(jnp.arange(n_tiles)[:, None] >= tile_end[None, :], axis=1),
                              N_EXPERTS - 1).astype(jnp.int32)
    n_used = tile_end[-1:].astype(jnp.int32)
    expert = route[:, ROUTE_I1:ROUTE_I2 + 1].astype(jnp.int32)
    rank = route[:, ROUTE_R1:ROUTE_R2 + 1].astype(jnp.int32)
    slots = (row_start[expert] + rank).reshape(n // tm, 1, 2 * tm)
    n_rows = n_tiles * tm_e
    pad_start = jnp.minimum((row_start + cnt) // 8 * 8, n_rows - tm_e)
    tail_start = jnp.minimum(tile_end[-1] + jnp.arange(N_EXPERTS), n_tiles - 1) * tm_e
    pad_start = jnp.concatenate([pad_start, tail_start]).astype(jnp.int32)
    xs = _dispatch(pad_start, slots, h, n_rows, tm, tm_e)
    ys = _experts(tile_expert, n_used, xs, wg, wu, wd, layer, tm_e, tf)
    return _combine(slots, ys, route, x, gate, seq, tm)


def _group_mean_matrix():
    g = np.zeros((HEAD_PAD, HEAD_PAD), np.float32)
    g[:NOPE_DIM, :NOPE_DIM] = 1.0 / NOPE_DIM
    g[NOPE_DIM:QK_DIM, NOPE_DIM:QK_DIM] = 1.0 / ROPE_DIM
    return jnp.asarray(g, dtype=BF16)


def _pad_cols(a, left, total):
    return jnp.pad(a, ((0, 0), (left, total - left - a.shape[1])))


def _layer_weights(l, w_in, w_q_up, w_kv_up, q_gain, k_gain, w_a, w_x):
    wi = w_in[l]
    n_lat = Q_RANK + KV_RANK
    wz = jnp.concatenate([wi[:, :n_lat], _pad_cols(wi[:, n_lat:n_lat + ROPE_DIM], NOPE_DIM, HEAD_PAD),
                          wi[:, n_lat + ROPE_DIM:]], axis=1).astype(BF16)
    wq = jnp.pad(w_q_up[l].reshape(Q_RANK, MLA_HEADS, QK_DIM),
                 ((0, 0), (0, 0), (0, HEAD_PAD - QK_DIM))).reshape(Q_RANK, QK_PAD).astype(BF16)
    wkv = w_kv_up[l].reshape(KV_RANK, MLA_HEADS, NOPE_DIM + V_DIM)
    wk = jnp.pad(wkv[..., :NOPE_DIM], ((0, 0), (0, 0), (0, HEAD_PAD - NOPE_DIM)))
    wk = wk.reshape(KV_RANK, QK_PAD).astype(BF16)
    wv = jnp.pad(wkv[..., NOPE_DIM:], ((0, 0), (0, 0), (0, HEAD_PAD - V_DIM)))
    wv = wv.reshape(KV_RANK, QK_PAD).astype(BF16)
    qg = _pad_cols(q_gain[l][None, :], 0, HEAD_PAD)
    kg = _pad_cols(k_gain[l][None, :NOPE_DIM], 0, HEAD_PAD)
    krg = _pad_cols(k_gain[l][None, NOPE_DIM:], NOPE_DIM, HEAD_PAD)
    eye = jnp.eye(LRU_BLOCKS, dtype=F32)
    blockdiag = lambda w: jnp.einsum('gij,gh->gihj', w, eye).reshape(LRU_WIDTH, LRU_WIDTH)
    wgate = jnp.concatenate([blockdiag(w_a[l]), blockdiag(w_x[l])], axis=1).astype(BF16)
    return wz, wq, wk, wv, qg, kg, krg, wgate


def kernel(x, c, positions, w_ada, b_ada, norm_gain, w_in, latent_gain, w_q_up, w_kv_up, q_gain, k_gain,
           conv_w, conv_b, w_a, b_a, w_x, b_x, lru_lambda, out_gain, w_out, ffn_w_gate, ffn_w_up,
           ffn_w_down, router_w, moe_w_gate, moe_w_up, moe_w_down):
    b, s, d = x.shape
    n = b * s
    tm = min(512, s)
    tq = min(512, s)
    tk = min(512, s)

    c_pad = jnp.pad(c, ((0, 8 - b), (0, 0)))
    mods = _ada_all(c_pad, w_ada.reshape(DEPTH * 2, d, 3 * d), b_ada.reshape(DEPTH * 2, 1, 3 * d))
    mods = mods[:, :b].reshape(DEPTH, 2, b, 1, 3, d)
    shift_of = lambda l, i: mods[l, i, :, :, 0]
    scale_of = lambda l, i: mods[l, i, :, :, 1]
    gate_of = lambda l, i: mods[l, i, :, :, 2]

    inv_freq = ROPE_THETA ** (-jnp.arange(0, ROPE_DIM, 2, dtype=F32) / ROPE_DIM)
    invf = _pad_cols(jnp.concatenate([inv_freq, inv_freq])[None, :], NOPE_DIM, HEAD_PAD)
    pos_col = positions.reshape(b, s, 1)
    pos_row = positions.reshape(b, 1, s)
    cos, s1, s2 = _rope_tables(pos_col, invf, tm)
    gm = _group_mean_matrix()

    for l in range(DEPTH):
        wz, wq, wk, wv, qg, kg, krg, wgate = _layer_weights(l, w_in, w_q_up, w_kv_up, q_gain, k_gain, w_a, w_x)
        q, k, v, xr, gr = _premix(x, cos, s1, s2, shift_of(l, 0), scale_of(l, 0), norm_gain[l, 0][None, :],
                                  wz, latent_gain[l][None, :], wq, wk, wv, gm, qg, kg, krg, tm)
        attn = _attention(q, k, v, pos_col, pos_row, tq, tk)
        rec = _lru(xr, gr, conv_w[l], conv_b[l][None, :], wgate, b_a[l][None, :], b_x[l][None, :],
                   lru_lambda[l][None, :], tm)
        j = l // 2
        moe = l % 2 == 1
        wr = jnp.pad(router_w[j], ((0, 0), (0, ROUTER_PAD - N_EXPERTS))) if moe else None
        outs = _postmix(attn, rec, x, gate_of(l, 0), out_gain[l][None, :], w_out[l].astype(BF16),
                        norm_gain[l, 1][None, :], shift_of(l, 1), scale_of(l, 1), wr, tm)
        x_mid, h2 = outs[0].reshape(n, d), outs[1].reshape(n, d)
        if moe:
            x = _moe(h2, outs[2].reshape(n, ROUTER_PAD), outs[3], moe_w_gate, moe_w_up, moe_w_down, j,
                     x_mid, gate_of(l, 1), s, tm, min(1024, n), 512)
        else:
            x = _ffn(h2, ffn_w_gate[j].astype(BF16), ffn_w_up[j].astype(BF16), ffn_w_down[j].astype(BF16),
                     x_mid, gate_of(l, 1), s, tm)
        x = x.reshape(b, s, d)
    return x
```

```python
import functools

import numpy as np
import jax
import jax.numpy as jnp
from jax import lax
from jax.experimental import pallas as pl
from jax.experimental.pallas import tpu as pltpu

F32 = jnp.float32
BF16 = jnp.bfloat16

D_MODEL = 1024
DEPTH = 4
MLA_HEADS = 8
NOPE_DIM = 64
ROPE_DIM = 32
HALF_ROPE = ROPE_DIM // 2
V_DIM = 64
QK_DIM = NOPE_DIM + ROPE_DIM
Q_RANK = 256
KV_RANK = 128
MLA_WIDTH = MLA_HEADS * V_DIM
LRU_WIDTH = D_MODEL - MLA_WIDTH
LRU_BLOCKS = 8
LRU_BLOCK = LRU_WIDTH // LRU_BLOCKS
CONV_WIDTH = 4
LRU_C = 8.0
N_EXPERTS = 8
ROPE_THETA = 10000.0
EPS = 1e-6

HEAD_PAD = 128
QK_PAD = MLA_HEADS * HEAD_PAD
Z_COLS = Q_RANK + KV_RANK + HEAD_PAD + 2 * LRU_WIDTH
ROUTER_PAD = 128
SM_SCALE = QK_DIM ** -0.5 * float(np.log2(np.e))
NEG = float(np.finfo(np.float32).min)
VMEM_LIMIT = 56 * 1024 * 1024


def _cparams(sem):
    return pltpu.CompilerParams(dimension_semantics=sem, vmem_limit_bytes=VMEM_LIMIT)


def _rms_rows(v):
    return v * lax.rsqrt(jnp.mean(v * v, axis=-1, keepdims=True) + EPS)


def _ada_kernel(c_ref, w_ref, b_ref, o_ref):
    c = c_ref[...]
    c_act = c * jax.nn.sigmoid(c)
    o_ref[0] = jnp.dot(c_act, w_ref[0], preferred_element_type=F32,
                       precision=lax.Precision.HIGHEST) + b_ref[0]


def _ada_all(c_pad, w_ada, b_ada):
    l2 = w_ada.shape[0]
    tn = 1536
    return pl.pallas_call(
        _ada_kernel,
        grid=(l2, 3 * D_MODEL // tn),
        in_specs=[pl.BlockSpec((8, D_MODEL), lambda l, n: (0, 0)),
                  pl.BlockSpec((1, D_MODEL, tn), lambda l, n: (l, 0, n)),
                  pl.BlockSpec((1, 1, tn), lambda l, n: (l, 0, n))],
        out_specs=pl.BlockSpec((1, 8, tn), lambda l, n: (l, 0, n)),
        out_shape=jax.ShapeDtypeStruct((l2, 8, 3 * D_MODEL), F32),
        compiler_params=_cparams(("parallel", "parallel")),
    )(c_pad, w_ada, b_ada)


def _rope_kernel(pos_ref, invf_ref, cos_ref, s1_ref, s2_ref):
    ang = pos_ref[0].astype(F32) * invf_ref[...]
    c = jnp.cos(ang)
    s = jnp.sin(ang)
    lane = lax.broadcasted_iota(jnp.int32, ang.shape, 1)
    first = (lane >= NOPE_DIM) & (lane < NOPE_DIM + HALF_ROPE)
    second = (lane >= NOPE_DIM + HALF_ROPE) & (lane < QK_DIM)
    cos_ref[0] = c
    s1_ref[0] = jnp.where(first, -s, 0.0)
    s2_ref[0] = jnp.where(second, s, 0.0)


def _rope_tables(pos3, invf, tm):
    b, s, _ = pos3.shape
    spec = pl.BlockSpec((1, tm, HEAD_PAD), lambda i, j: (i, j, 0))
    shp = jax.ShapeDtypeStruct((b, s, HEAD_PAD), F32)
    return pl.pallas_call(
        _rope_kernel,
        grid=(b, s // tm),
        in_specs=[pl.BlockSpec((1, tm, 1), lambda i, j: (i, j, 0)),
                  pl.BlockSpec((1, HEAD_PAD), lambda i, j: (0, 0))],
        out_specs=[spec, spec, spec],
        out_shape=[shp, shp, shp],
        compiler_params=_cparams(("parallel", "parallel")),
    )(pos3, invf)


def _premix_kernel(x_ref, cos_ref, s1_ref, s2_ref, shift_ref, scale_ref, ng_ref, wz_ref, lg_ref,
                   wq_ref, wk_ref, wv_ref, gm_ref, qg_ref, kg_ref, krg_ref,
                   q_out, k_out, v_out, xr_out, gr_out):
    x = x_ref[0]
    h = _rms_rows(x) * ng_ref[...]
    h = h * (1.0 + scale_ref[0]) + shift_ref[0]
    z = jnp.dot(h.astype(BF16), wz_ref[...], preferred_element_type=F32)
    xr_out[0] = z[:, Q_RANK + KV_RANK + HEAD_PAD:Q_RANK + KV_RANK + HEAD_PAD + LRU_WIDTH]
    gr_out[0] = z[:, Q_RANK + KV_RANK + HEAD_PAD + LRU_WIDTH:]

    lg = lg_ref[...]
    cq = _rms_rows(z[:, :Q_RANK]) * lg[:, :Q_RANK]
    ckv = (_rms_rows(z[:, Q_RANK:Q_RANK + KV_RANK]) * lg[:, Q_RANK:]).astype(BF16)
    kr = z[:, Q_RANK + KV_RANK:Q_RANK + KV_RANK + HEAD_PAD]

    qf = jnp.dot(cq.astype(BF16), wq_ref[...], preferred_element_type=F32)
    kf = jnp.dot(ckv, wk_ref[...], preferred_element_type=F32)
    vf = jnp.dot(ckv, wv_ref[...], preferred_element_type=F32)
    vlane = lax.broadcasted_iota(jnp.int32, vf.shape, 1)
    v_out[0] = jnp.where(vlane % HEAD_PAD == V_DIM, 1.0, vf).astype(BF16)

    cos = cos_ref[0]
    s1 = s1_ref[0]
    s2 = s2_ref[0]
    gm = gm_ref[...]

    def group_norm(blk):
        msq = jnp.dot((blk * blk).astype(BF16), gm, preferred_element_type=F32)
        return blk * lax.rsqrt(msq + EPS)

    def rope(blk):
        return (blk * cos + pltpu.roll(blk, HEAD_PAD - HALF_ROPE, 1) * s1
                + pltpu.roll(blk, HALF_ROPE, 1) * s2)

    kr_rot = rope(group_norm(kr) * krg_ref[...])
    qg = qg_ref[...]
    kg = kg_ref[...]
    for hd in range(MLA_HEADS):
        sl = slice(hd * HEAD_PAD, (hd + 1) * HEAD_PAD)
        qh = rope(group_norm(qf[:, sl]) * qg) * SM_SCALE
        q_out[0, :, sl] = qh.astype(BF16)
        kh = group_norm(kf[:, sl]) * kg + kr_rot
        k_out[0, :, sl] = kh.astype(BF16)


def _premix(x, cos, s1, s2, shift, scale, ng, wz, lgain, wq, wk, wv, gm, qg, kg, krg, tm):
    b, s, d = x.shape
    tok = lambda w: pl.BlockSpec((1, tm, w), lambda i, j: (i, j, 0))
    per_b = pl.BlockSpec((1, 1, d), lambda i, j: (i, 0, 0))
    full = lambda a: pl.BlockSpec(a.shape, lambda i, j: (0,) * a.ndim)
    return pl.pallas_call(
        _premix_kernel,
        grid=(b, s // tm),
        in_specs=[tok(d), tok(HEAD_PAD), tok(HEAD_PAD), tok(HEAD_PAD), per_b, per_b, full(ng), full(wz),
                  full(lgain), full(wq), full(wk), full(wv), full(gm), full(qg), full(kg), full(krg)],
        out_specs=[tok(QK_PAD), tok(QK_PAD), tok(QK_PAD), tok(LRU_WIDTH), tok(LRU_WIDTH)],
        out_shape=[jax.ShapeDtypeStruct((b, s, QK_PAD), BF16),
                   jax.ShapeDtypeStruct((b, s, QK_PAD), BF16),
                   jax.ShapeDtypeStruct((b, s, QK_PAD), BF16),
                   jax.ShapeDtypeStruct((b, s, LRU_WIDTH), F32),
                   jax.ShapeDtypeStruct((b, s, LRU_WIDTH), F32)],
        compiler_params=_cparams(("parallel", "parallel")),
    )(x, cos, s1, s2, shift, scale, ng, wz, lgain, wq, wk, wv, gm, qg, kg, krg)


SUBLANES = 8
ATTN_ROWS = 32
ATTN_HEADS = 4
LANES = 128


def _attn_kernel(q_ref, k_ref, v_ref, pq_ref, pk_ref, o_ref, s_scr, p_scr, m_scr, acc_scr, pq_scr, *, tq, tk):
    assert tq == tk
    qi = pl.program_id(2)
    m_scr[...] = jnp.full(m_scr.shape, NEG, F32)
    acc_scr[...] = jnp.zeros(acc_scr.shape, F32)
    pq_scr[...] = jnp.broadcast_to(pq_ref[0], pq_scr.shape)
    head_lanes = [slice(hh * HEAD_PAD, (hh + 1) * HEAD_PAD) for hh in range(ATTN_HEADS)]

    def chunk(j, width, masked):
        ks = pl.multiple_of(j * tk, tk)
        for hh, sl in enumerate(head_lanes):
            s_scr[hh, :, 0:width] = lax.dot_general(q_ref[0, :, sl], k_ref[0, pl.ds(ks, width), sl],
                                                    (((1,), (1,)), ((), ())), preferred_element_type=F32)
        for hh, sl in enumerate(head_lanes):
            for rb in range(tq // ATTN_ROWS):
                rows = slice(rb * ATTN_ROWS, (rb + 1) * ATTN_ROWS)

                first_row, last_row = rb * ATTN_ROWS, (rb + 1) * ATTN_ROWS - 1
                visible = [not masked or (cb + 1) * LANES - 1 <= first_row for cb in range(width // LANES)]
                hidden = [masked and cb * LANES > last_row for cb in range(width // LANES)]

                def block(cb):
                    blk = s_scr[hh, rows, cb * LANES:(cb + 1) * LANES]
                    if not visible[cb]:
                        keep = pk_ref[0, :, pl.ds(ks + cb * LANES, LANES)] <= pq_scr[rows, :]
                        blk = jnp.where(keep, blk, NEG)
                    return blk

                live = [cb for cb in range(width // LANES) if not hidden[cb]]
                rmax = block(live[0])
                for cb in live[1:]:
                    rmax = jnp.maximum(rmax, block(cb))
                m_old = m_scr[hh, rows, :]
                m_new = jnp.maximum(m_old, jnp.max(rmax, axis=-1, keepdims=True))
                m_scr[hh, rows, :] = m_new
                acc_scr[hh, rows, :] = acc_scr[hh, rows, :] * jnp.exp2(m_old - m_new)
                for cb in range(width // LANES):
                    if hidden[cb]:
                        p_blk = jnp.zeros((ATTN_ROWS, LANES), BF16)
                    else:
                        p_blk = jnp.exp2(block(cb) - m_new).astype(BF16)
                    p_scr[hh, rows, cb * LANES:(cb + 1) * LANES] = p_blk
            acc_scr[hh] += jnp.dot(p_scr[hh, :, 0:width], v_ref[0, pl.ds(ks, width), sl],
                                   preferred_element_type=F32)

    def two_chunks(i, carry):
        chunk(2 * i, 2 * tk, False)
        return carry

    lax.fori_loop(0, qi // 2, two_chunks, 0)

    @pl.when(qi % 2 == 1)
    def _():
        chunk(qi - 1, tk, False)

    chunk(qi, tk, True)

    lane = lax.broadcasted_iota(jnp.int32, (tq, LANES), 1)
    for pair in range(ATTN_HEADS // 2):
        outs = []
        for hh in (2 * pair, 2 * pair + 1):
            acc = acc_scr[hh]
            den = jnp.sum(jnp.where(lane == V_DIM, acc, 0.0), axis=-1, keepdims=True)
            outs.append(acc / den)
        o_ref[0, :, pair * LANES:(pair + 1) * LANES] = jnp.where(
            lane < V_DIM, outs[0], pltpu.roll(outs[1], V_DIM, 1)).astype(o_ref.dtype)


def _attention(q, k, v, pos_col, pos_row, tq, tk):
    b, s, _ = q.shape
    groups = MLA_HEADS // ATTN_HEADS
    width = ATTN_HEADS * HEAD_PAD
    resident = lambda: pl.BlockSpec((1, s, width), lambda i, g, j: (i, 0, g), pipeline_mode=pl.Buffered(1))
    return pl.pallas_call(
        functools.partial(_attn_kernel, tq=tq, tk=tk),
        grid=(b, groups, s // tq),
        in_specs=[pl.BlockSpec((1, tq, width), lambda i, g, j: (i, j, g)),
                  resident(), resident(),
                  pl.BlockSpec((1, tq, 1), lambda i, g, j: (i, j, 0)),
                  pl.BlockSpec((1, 1, s), lambda i, g, j: (i, 0, 0))],
        out_specs=pl.BlockSpec((1, tq, ATTN_HEADS * V_DIM), lambda i, g, j: (i, j, g)),
        out_shape=jax.ShapeDtypeStruct((b, s, MLA_WIDTH), BF16),
        scratch_shapes=[pltpu.VMEM((ATTN_HEADS, tq, 2 * tk), F32), pltpu.VMEM((ATTN_HEADS, tq, 2 * tk), BF16),
                        pltpu.VMEM((ATTN_HEADS, tq, LANES), F32), pltpu.VMEM((ATTN_HEADS, tq, LANES), F32),
                        pltpu.VMEM((tq, LANES), jnp.int32)],
        compiler_params=_cparams(("parallel", "parallel", "arbitrary")),
    )(q, k, v, pos_col, pos_row)


def _gelu_tanh(x):
    return 0.5 * x * (1.0 + jnp.tanh(np.sqrt(2.0 / np.pi).astype(np.float32) * (x + 0.044715 * (x * x * x))))


def _lru_kernel(xr_ref, gr_ref, cw_ref, cb_ref, wg_ref, ba_ref, bx_ref, lam_ref, o_ref, xbuf, hc, *, tm):
    ti = pl.program_id(1)
    halo = SUBLANES

    @pl.when(ti == 0)
    def _():
        xbuf[0:halo, :] = jnp.zeros((halo, LRU_WIDTH), F32)
        hc[...] = jnp.zeros_like(hc)

    @pl.when(ti > 0)
    def _():
        xbuf[0:halo, :] = xbuf[tm:tm + halo, :]

    xbuf[halo:tm + halo, :] = xr_ref[0]
    cw = cw_ref[...]
    xc = cb_ref[...] + cw[CONV_WIDTH - 1:CONV_WIDTH, :] * xbuf[halo:tm + halo, :]
    for tap in range(1, CONV_WIDTH):
        xc = xc + cw[CONV_WIDTH - 1 - tap:CONV_WIDTH - tap, :] * xbuf[halo - tap:tm + halo - tap, :]

    gz = jnp.dot(xc.astype(BF16), wg_ref[...], preferred_element_type=F32)
    r = jax.nn.sigmoid(gz[:, :LRU_WIDTH] + ba_ref[...])
    gi = jax.nn.sigmoid(gz[:, LRU_WIDTH:] + bx_ref[...])
    lam = lam_ref[...]
    log_sig = -(jnp.maximum(-lam, 0.0) + jnp.log1p(jnp.exp(-jnp.abs(lam))))
    log_a = LRU_C * r * log_sig
    a = jnp.exp(log_a)
    u = jnp.sqrt(-jnp.tanh(log_a) * (a * a + 1.0)) * (gi * xc)

    sub = lax.broadcasted_iota(jnp.int32, (tm, LRU_WIDTH), 0) % SUBLANES
    d = 1
    while d < SUBLANES:
        keep = sub >= d
        a_prev = jnp.where(keep, pltpu.roll(a, d, 0), 1.0)
        u_prev = jnp.where(keep, pltpu.roll(u, d, 0), 0.0)
        u = a * u_prev + u
        a = a * a_prev
        d *= 2
    h_in = hc[...]
    groups = []
    for g in range(tm // SUBLANES):
        rows = slice(g * SUBLANES, (g + 1) * SUBLANES)
        blk = a[rows, :] * h_in + u[rows, :]
        groups.append(blk)
        h_in = blk[SUBLANES - 1:SUBLANES, :]
    hc[...] = h_in
    hseq = jnp.concatenate(groups, axis=0)
    o_ref[0] = (hseq * _gelu_tanh(gr_ref[0])).astype(o_ref.dtype)


def _lru(xr, gr, cw, cb, wg, ba, bx, lam, tm):
    b, s, w = xr.shape
    tok = pl.BlockSpec((1, tm, w), lambda i, j: (i, j, 0))
    full = lambda a: pl.BlockSpec(a.shape, lambda i, j: (0,) * a.ndim)
    return pl.pallas_call(
        functools.partial(_lru_kernel, tm=tm),
        grid=(b, s // tm),
        in_specs=[tok, tok, full(cw), full(cb), full(wg), full(ba), full(bx), full(lam)],
        out_specs=tok,
        out_shape=jax.ShapeDtypeStruct((b, s, w), BF16),
        scratch_shapes=[pltpu.VMEM((tm + SUBLANES, w), F32), pltpu.VMEM((1, w), F32)],
        compiler_params=_cparams(("parallel", "arbitrary")),
    )(xr, gr, cw, cb, wg, ba, bx, lam)


ROUTE_I1, ROUTE_I2, ROUTE_W1, ROUTE_W2, ROUTE_R1, ROUTE_R2 = range(6)


def _postmix_kernel(*refs, router):
    if router:
        (a_ref, r_ref, x_ref, gate_ref, og_ref, wo_ref, ng_ref, shift_ref, scale_ref, wr_ref, tri_ref,
         x_out, h_out, route_out, count_out, count_scr) = refs
    else:
        (a_ref, r_ref, x_ref, gate_ref, og_ref, wo_ref, ng_ref, shift_ref, scale_ref,
         x_out, h_out) = refs
    og = og_ref[...]
    ya = (_rms_rows(a_ref[0].astype(F32)) * og[:, :MLA_WIDTH]).astype(BF16)
    yr = (_rms_rows(r_ref[0].astype(F32)) * og[:, MLA_WIDTH:]).astype(BF16)
    y = (jnp.dot(ya, wo_ref[0:MLA_WIDTH, :], preferred_element_type=F32)
         + jnp.dot(yr, wo_ref[MLA_WIDTH:, :], preferred_element_type=F32))
    xn = x_ref[0] + gate_ref[0] * y
    x_out[0] = xn
    h2 = _rms_rows(xn) * ng_ref[...]
    h2 = h2 * (1.0 + scale_ref[0]) + shift_ref[0]
    h_out[0] = h2.astype(h_out.dtype)
    if router:
        @pl.when((pl.program_id(0) == 0) & (pl.program_id(1) == 0))
        def _():
            count_scr[...] = jnp.zeros_like(count_scr)

        h_hi = h2.astype(BF16)
        h_lo = (h2 - h_hi.astype(F32)).astype(BF16)
        wr = wr_ref[...]
        w_hi = wr.astype(BF16)
        w_lo = (wr - w_hi.astype(F32)).astype(BF16)
        logits = (jnp.dot(h_hi, w_hi, preferred_element_type=F32)
                  + (jnp.dot(h_hi, w_lo, preferred_element_type=F32)
                     + jnp.dot(h_lo, w_hi, preferred_element_type=F32)))
        lane = lax.broadcasted_iota(jnp.int32, logits.shape, 1)
        lg = jnp.where(lane < N_EXPERTS, logits, -jnp.inf)
        m1 = jnp.max(lg, axis=-1, keepdims=True)
        i1 = jnp.min(jnp.where(lg == m1, lane, ROUTER_PAD), axis=-1, keepdims=True)
        lg2 = jnp.where(lane == i1, -jnp.inf, lg)
        m2 = jnp.max(lg2, axis=-1, keepdims=True)
        i2 = jnp.min(jnp.where(lg2 == m2, lane, ROUTER_PAD), axis=-1, keepdims=True)
        e2 = jnp.exp(m2 - m1)
        den = 1.0 + e2
        sel1 = lane == i1
        sel2 = lane == i2
        sel = jnp.where(sel1, 1.0, jnp.where(sel2, 1.0, 0.0))
        before = count_scr[...] + jnp.dot(tri_ref[...], sel.astype(BF16), preferred_element_type=F32)
        r1 = jnp.sum(jnp.where(sel1, before, 0.0), axis=-1, keepdims=True)
        r2 = jnp.sum(jnp.where(sel2, before, 0.0), axis=-1, keepdims=True)
        count_scr[...] += jnp.sum(sel, axis=0, keepdims=True)
        count_out[...] = count_scr[...]
        rec = jnp.zeros(logits.shape, F32)
        for ln, val in ((ROUTE_I1, i1.astype(F32)), (ROUTE_I2, i2.astype(F32)), (ROUTE_W1, 1.0 / den),
                        (ROUTE_W2, e2 / den), (ROUTE_R1, r1), (ROUTE_R2, r2)):
            rec = jnp.where(lane == ln, val, rec)
        route_out[0] = rec


def _postmix(attn, rec, x, gate, og, wo, ng, shift, scale, wr, tm):
    b, s, d = x.shape
    router = wr is not None
    tok = lambda w: pl.BlockSpec((1, tm, w), lambda i, j: (i, j, 0))
    per_b = pl.BlockSpec((1, 1, d), lambda i, j: (i, 0, 0))
    full = lambda a: pl.BlockSpec(a.shape, lambda i, j: (0,) * a.ndim)
    in_specs = [tok(MLA_WIDTH), tok(LRU_WIDTH), tok(d), per_b, full(og), full(wo), full(ng), per_b, per_b]
    args = [attn, rec, x, gate, og, wo, ng, shift, scale]
    out_specs = [tok(d), tok(d)]
    out_shape = [jax.ShapeDtypeStruct((b, s, d), F32), jax.ShapeDtypeStruct((b, s, d), F32 if router else BF16)]
    scratch = []
    if router:
        tri = jnp.asarray(np.tril(np.ones((tm, tm), np.float32), -1), dtype=BF16)
        in_specs += [full(wr), full(tri)]
        args += [wr, tri]
        out_specs += [tok(ROUTER_PAD), pl.BlockSpec((1, ROUTER_PAD), lambda i, j: (0, 0))]
        out_shape += [jax.ShapeDtypeStruct((b, s, ROUTER_PAD), F32), jax.ShapeDtypeStruct((1, ROUTER_PAD), F32)]
        scratch = [pltpu.VMEM((1, ROUTER_PAD), F32)]
    return pl.pallas_call(
        functools.partial(_postmix_kernel, router=router),
        grid=(b, s // tm),
        in_specs=in_specs, out_specs=out_specs, out_shape=out_shape, scratch_shapes=scratch,
        compiler_params=_cparams(("arbitrary", "arbitrary") if router else ("parallel", "parallel")),
    )(*args)


FF_SUB = 256


def _swiglu_chunk(h, wg_ref, wu_ref, wd_ref):
    out = None
    for c in range(wg_ref.shape[1] // FF_SUB):
        cols = slice(c * FF_SUB, (c + 1) * FF_SUB)
        g = jnp.dot(h, wg_ref[:, cols].astype(BF16), preferred_element_type=F32)
        u = jnp.dot(h, wu_ref[:, cols].astype(BF16), preferred_element_type=F32)
        act = (g * jax.nn.sigmoid(g)) * u
        part = jnp.dot(act.astype(BF16), wd_ref[cols, :].astype(BF16), preferred_element_type=F32)
        out = part if out is None else out + part
    return out


def _ffn_kernel(h_ref, wg_ref, wu_ref, wd_ref, x_ref, gate_ref, o_ref):
    o_ref[...] = x_ref[...] + gate_ref[0] * _swiglu_chunk(h_ref[...], wg_ref, wu_ref, wd_ref)


def _ffn(h, wg, wu, wd, x, gate, seq, tm):
    n, d = x.shape
    tiles_per_seq = seq // tm
    resident = lambda a: pl.BlockSpec(a.shape, lambda i: (0, 0), pipeline_mode=pl.Buffered(1))
    return pl.pallas_call(
        _ffn_kernel,
        grid=(n // tm,),
        in_specs=[pl.BlockSpec((tm, d), lambda i: (i, 0)),
                  resident(wg), resident(wu), resident(wd),
                  pl.BlockSpec((tm, d), lambda i: (i, 0)),
                  pl.BlockSpec((1, 1, d), lambda i: (i // tiles_per_seq, 0, 0))],
        out_specs=pl.BlockSpec((tm, d), lambda i: (i, 0)),
        out_shape=jax.ShapeDtypeStruct((n, d), F32),
        compiler_params=_cparams(("parallel",)),
    )(h, wg, wu, wd, x, gate)


ROW_UNROLL = 8


def _start_row_copies(tm, copy_of):
    def body(i, c):
        for u in range(ROW_UNROLL):
            for k in range(2):
                copy_of(i * ROW_UNROLL + u, k).start(priority=k)
        return c

    lax.fori_loop(0, tm // ROW_UNROLL, body, 0)


def _wait_bytes_of(ref, sem, times):
    for _ in range(times):
        pltpu.make_async_copy(ref, ref, sem).wait()


def _dispatch_kernel(pad_ref, slot_ref, h_ref, xs_ref, zbuf, sem, *, tm, tm_e):
    @pl.when(pl.program_id(0) == 0)
    def _():
        zbuf[...] = jnp.zeros_like(zbuf)
        for e in range(2 * N_EXPERTS):
            pad = pl.multiple_of(pad_ref[e], 8)
            zero_copy = pltpu.make_async_copy(zbuf, xs_ref.at[pl.ds(pad, tm_e), :], sem)
            zero_copy.start()
            zero_copy.wait()

    _start_row_copies(tm, lambda r, k: pltpu.make_async_copy(
        h_ref.at[pl.ds(r, 1), :], xs_ref.at[pl.ds(slot_ref[0, 0, 2 * r + k], 1), :], sem))
    _wait_bytes_of(h_ref, sem, 2)


def _dispatch(pad_start, slots, h, n_rows, tm, tm_e):
    n, d = h.shape
    grid_spec = pltpu.PrefetchScalarGridSpec(
        num_scalar_prefetch=1,
        grid=(n // tm,),
        in_specs=[pl.BlockSpec((1, 1, 2 * tm), lambda i, pad: (i, 0, 0), memory_space=pltpu.SMEM),
                  pl.BlockSpec((tm, d), lambda i, pad: (i, 0))],
        out_specs=pl.BlockSpec(memory_space=pl.ANY),
        scratch_shapes=[pltpu.VMEM((tm_e, d), F32), pltpu.SemaphoreType.DMA(())])
    return pl.pallas_call(
        functools.partial(_dispatch_kernel, tm=tm, tm_e=tm_e),
        grid_spec=grid_spec,
        out_shape=jax.ShapeDtypeStruct((n_rows, d), F32),
        compiler_params=_cparams(("arbitrary",)),
    )(pad_start, slots, h)


def _expert_kernel(te_ref, nu_ref, x_ref, wg_ref, wu_ref, wd_ref, o_ref, acc, *, n_f):
    del te_ref
    i = pl.program_id(0)
    f = pl.program_id(1)

    @pl.when(f == 0)
    def _():
        acc[...] = jnp.zeros_like(acc)

    @pl.when(i < nu_ref[0])
    def _():
        acc[...] += _swiglu_chunk(x_ref[...].astype(BF16), wg_ref.at[0, 0], wu_ref.at[0, 0], wd_ref.at[0, 0])

    @pl.when(f == n_f - 1)
    def _():
        o_ref[...] = acc[...]


def _experts(tile_expert, n_used, xs, wg, wu, wd, layer, tm, tf):
    p, d = xs.shape
    n_f = wg.shape[3] // tf
    f_of = lambda i, f, nu: jnp.where(i < nu[0], f, n_f - 1)
    grid_spec = pltpu.PrefetchScalarGridSpec(
        num_scalar_prefetch=2,
        grid=(p // tm, n_f),
        in_specs=[pl.BlockSpec((tm, d), lambda i, f, te, nu: (jnp.minimum(i, nu[0] - 1), 0)),
                  pl.BlockSpec((1, 1, d, tf), lambda i, f, te, nu: (layer, te[i], 0, f_of(i, f, nu))),
                  pl.BlockSpec((1, 1, d, tf), lambda i, f, te, nu: (layer, te[i], 0, f_of(i, f, nu))),
                  pl.BlockSpec((1, 1, tf, d), lambda i, f, te, nu: (layer, te[i], f_of(i, f, nu), 0))],
        out_specs=pl.BlockSpec((tm, d), lambda i, f, te, nu: (i, 0)),
        scratch_shapes=[pltpu.VMEM((tm, d), F32)])
    return pl.pallas_call(
        functools.partial(_expert_kernel, n_f=n_f),
        grid_spec=grid_spec,
        out_shape=jax.ShapeDtypeStruct((p, d), F32),
        compiler_params=_cparams(("arbitrary", "arbitrary")),
    )(tile_expert, n_used, xs, wg, wu, wd)


def _combine_kernel(slot_ref, ys_ref, route_ref, x_ref, gate_ref, o_ref, buf, sem, *, tm):
    _start_row_copies(tm, lambda r, k: pltpu.make_async_copy(
        ys_ref.at[pl.ds(slot_ref[0, 0, 2 * r + k], 1), :], buf.at[k, pl.ds(r, 1), :], sem))
    _wait_bytes_of(buf.at[0], sem, 2)
    route = route_ref[...]
    w1 = route[:, ROUTE_W1:ROUTE_W1 + 1]
    w2 = route[:, ROUTE_W2:ROUTE_W2 + 1]
    o_ref[...] = x_ref[...] + gate_ref[0] * (w1 * buf[0] + w2 * buf[1])


def _combine(slots, ys, route, x, gate, seq, tm):
    n, d = x.shape
    tiles_per_seq = seq // tm
    return pl.pallas_call(
        functools.partial(_combine_kernel, tm=tm),
        grid=(n // tm,),
        in_specs=[pl.BlockSpec((1, 1, 2 * tm), lambda i: (i, 0, 0), memory_space=pltpu.SMEM),
                  pl.BlockSpec(memory_space=pl.ANY),
                  pl.BlockSpec((tm, ROUTER_PAD), lambda i: (i, 0)),
                  pl.BlockSpec((tm, d), lambda i: (i, 0)),
                  pl.BlockSpec((1, 1, d), lambda i: (i // tiles_per_seq, 0, 0))],
        out_specs=pl.BlockSpec((tm, d), lambda i: (i, 0)),
        out_shape=jax.ShapeDtypeStruct((n, d), F32),
        scratch_shapes=[pltpu.VMEM((2, tm, d), F32), pltpu.SemaphoreType.DMA(())],
        compiler_params=_cparams(("arbitrary",)),
    )(slots, ys, route, x, gate)


def _moe(h, route, counts, wg, wu, wd, layer, x, gate, seq, tm, tm_e, tf):
    n, d = x.shape
    cnt = counts[0, :N_EXPERTS].astype(jnp.int32)
    tiles = (cnt + tm_e - 1) // tm_e
    tile_end = jnp.cumsum(tiles)
    row_start = (tile_end - tiles) * tm_e
    n_tiles = 2 * n // tm_e + N_EXPERTS
    tile_expert = jnp.minimum(jnp.sum(jnp.arange(n_tiles)[:, None] >= tile_end[None, :], axis=1),
                              N_EXPERTS - 1).astype(jnp.int32)
    n_used = tile_end[-1:].astype(jnp.int32)
    expert = route[:, ROUTE_I1:ROUTE_I2 + 1].astype(jnp.int32)
    rank = route[:, ROUTE_R1:ROUTE_R2 + 1].astype(jnp.int32)
    slots = (row_start[expert] + rank).reshape(n // tm, 1, 2 * tm)
    n_rows = n_tiles * tm_e
    pad_start = jnp.minimum((row_start + cnt) // 8 * 8, n_rows - tm_e)
    tail_start = jnp.minimum(tile_end[-1] + jnp.arange(N_EXPERTS), n_tiles - 1) * tm_e
    pad_start = jnp.concatenate([pad_start, tail_start]).astype(jnp.int32)
    xs = _dispatch(pad_start, slots, h, n_rows, tm, tm_e)
    ys = _experts(tile_expert, n_used, xs, wg, wu, wd, layer, tm_e, tf)
    return _combine(slots, ys, route, x, gate, seq, tm)


def _group_mean_matrix():
    g = np.zeros((HEAD_PAD, HEAD_PAD), np.float32)
    g[:NOPE_DIM, :NOPE_DIM] = 1.0 / NOPE_DIM
    g[NOPE_DIM:QK_DIM, NOPE_DIM:QK_DIM] = 1.0 / ROPE_DIM
    return jnp.asarray(g, dtype=BF16)


def _pad_cols(a, left, total):
    return jnp.pad(a, ((0, 0), (left, total - left - a.shape[1])))


def _layer_weights(l, w_in, w_q_up, w_kv_up, q_gain, k_gain, w_a, w_x):
    wi = w_in[l]
    n_lat = Q_RANK + KV_RANK
    wz = jnp.concatenate([wi[:, :n_lat], _pad_cols(wi[:, n_lat:n_lat + ROPE_DIM], NOPE_DIM, HEAD_PAD),
                          wi[:, n_lat + ROPE_DIM:]], axis=1).astype(BF16)
    wq = jnp.pad(w_q_up[l].reshape(Q_RANK, MLA_HEADS, QK_DIM),
                 ((0, 0), (0, 0), (0, HEAD_PAD - QK_DIM))).reshape(Q_RANK, QK_PAD).astype(BF16)
    wkv = w_kv_up[l].reshape(KV_RANK, MLA_HEADS, NOPE_DIM + V_DIM)
    wk = jnp.pad(wkv[..., :NOPE_DIM], ((0, 0), (0, 0), (0, HEAD_PAD - NOPE_DIM)))
    wk = wk.reshape(KV_RANK, QK_PAD).astype(BF16)
    wv = jnp.pad(wkv[..., NOPE_DIM:], ((0, 0), (0, 0), (0, HEAD_PAD - V_DIM)))
    wv = wv.reshape(KV_RANK, QK_PAD).astype(BF16)
    qg = _pad_cols(q_gain[l][None, :], 0, HEAD_PAD)
    kg = _pad_cols(k_gain[l][None, :NOPE_DIM], 0, HEAD_PAD)
    krg = _pad_cols(k_gain[l][None, NOPE_DIM:], NOPE_DIM, HEAD_PAD)
    eye = jnp.eye(LRU_BLOCKS, dtype=F32)
    blockdiag = lambda w: jnp.einsum('gij,gh->gihj', w, eye).reshape(LRU_WIDTH, LRU_WIDTH)
    wgate = jnp.concatenate([blockdiag(w_a[l]), blockdiag(w_x[l])], axis=1).astype(BF16)
    return wz, wq, wk, wv, qg, kg, krg, wgate


def kernel(x, c, positions, w_ada, b_ada, norm_gain, w_in, latent_gain, w_q_up, w_kv_up, q_gain, k_gain,
           conv_w, conv_b, w_a, b_a, w_x, b_x, lru_lambda, out_gain, w_out, ffn_w_gate, ffn_w_up,
           ffn_w_down, router_w, moe_w_gate, moe_w_up, moe_w_down):
    b, s, d = x.shape
    n = b * s
    tm = min(512, s)
    tq = min(512, s)
    tk = min(512, s)

    c_pad = jnp.pad(c, ((0, 8 - b), (0, 0)))
    mods = _ada_all(c_pad, w_ada.reshape(DEPTH * 2, d, 3 * d), b_ada.reshape(DEPTH * 2, 1, 3 * d))
    mods = mods[:, :b].reshape(DEPTH, 2, b, 1, 3, d)
    shift_of = lambda l, i: mods[l, i, :, :, 0]
    scale_of = lambda l, i: mods[l, i, :, :, 1]
    gate_of = lambda l, i: mods[l, i, :, :, 2]

    inv_freq = ROPE_THETA ** (-jnp.arange(0, ROPE_DIM, 2, dtype=F32) / ROPE_DIM)
    invf = _pad_cols(jnp.concatenate([inv_freq, inv_freq])[None, :], NOPE_DIM, HEAD_PAD)
    pos_col = positions.reshape(b, s, 1)
    pos_row = positions.reshape(b, 1, s)
    cos, s1, s2 = _rope_tables(pos_col, invf, tm)
    gm = _group_mean_matrix()

    for l in range(DEPTH):
        wz, wq, wk, wv, qg, kg, krg, wgate = _layer_weights(l, w_in, w_q_up, w_kv_up, q_gain, k_gain, w_a, w_x)
        q, k, v, xr, gr = _premix(x, cos, s1, s2, shift_of(l, 0), scale_of(l, 0), norm_gain[l, 0][None, :],
                                  wz, latent_gain[l][None, :], wq, wk, wv, gm, qg, kg, krg, tm)
        attn = _attention(q, k, v, pos_col, pos_row, tq, tk)
        rec = _lru(xr, gr, conv_w[l], conv_b[l][None, :], wgate, b_a[l][None, :], b_x[l][None, :],
                   lru_lambda[l][None, :], tm)
        j = l // 2
        moe = l % 2 == 1
        wr = jnp.pad(router_w[j], ((0, 0), (0, ROUTER_PAD - N_EXPERTS))) if moe else None
        outs = _postmix(attn, rec, x, gate_of(l, 0), out_gain[l][None, :], w_out[l].astype(BF16),
                        norm_gain[l, 1][None, :], shift_of(l, 1), scale_of(l, 1), wr, tm)
        x_mid, h2 = outs[0].reshape(n, d), outs[1].reshape(n, d)
        if moe:
            x = _moe(h2, outs[2].reshape(n, ROUTER_PAD), outs[3], moe_w_gate, moe_w_up, moe_w_down, j,
                     x_mid, gate_of(l, 1), s, tm, min(1024, n), 512)
        else:
            x = _ffn(h2, ffn_w_gate[j].astype(BF16), ffn_w_up[j].astype(BF16), ffn_w_down[j].astype(BF16),
                     x_mid, gate_of(l, 1), s, tm)
        x = x.reshape(b, s, d)
    return x
```

```python
import functools

import numpy as np
import jax
import jax.numpy as jnp
from jax import lax
from jax.experimental import pallas as pl
from jax.experimental.pallas import tpu as pltpu

F32 = jnp.float32
BF16 = jnp.bfloat16

D_MODEL = 1024
DEPTH = 4
MLA_HEADS = 8
NOPE_DIM = 64
ROPE_DIM = 32
HALF_ROPE = ROPE_DIM // 2
V_DIM = 64
QK_DIM = NOPE_DIM + ROPE_DIM
Q_RANK = 256
KV_RANK = 128
MLA_WIDTH = MLA_HEADS * V_DIM
LRU_WIDTH = D_MODEL - MLA_WIDTH
LRU_BLOCKS = 8
LRU_BLOCK = LRU_WIDTH // LRU_BLOCKS
CONV_WIDTH = 4
LRU_C = 8.0
N_EXPERTS = 8
ROPE_THETA = 10000.0
EPS = 1e-6

HEAD_PAD = 128
QK_PAD = MLA_HEADS * HEAD_PAD
Z_COLS = Q_RANK + KV_RANK + HEAD_PAD + 2 * LRU_WIDTH
ROUTER_PAD = 128
SM_SCALE = QK_DIM ** -0.5 * float(np.log2(np.e))
NEG = float(np.finfo(np.float32).min)
VMEM_LIMIT = 56 * 1024 * 1024


def _cparams(sem):
    return pltpu.CompilerParams(dimension_semantics=sem, vmem_limit_bytes=VMEM_LIMIT)


def _rms_rows(v):
    return v * lax.rsqrt(jnp.mean(v * v, axis=-1, keepdims=True) + EPS)


def _ada_kernel(c_ref, w_ref, b_ref, o_ref):
    c = c_ref[...]
    c_act = c * jax.nn.sigmoid(c)
    o_ref[0] = jnp.dot(c_act, w_ref[0], preferred_element_type=F32,
                       precision=lax.Precision.HIGHEST) + b_ref[0]


def _ada_all(c_pad, w_ada, b_ada):
    l2 = w_ada.shape[0]
    tn = 1536
    return pl.pallas_call(
        _ada_kernel,
        grid=(l2, 3 * D_MODEL // tn),
        in_specs=[pl.BlockSpec((8, D_MODEL), lambda l, n: (0, 0)),
                  pl.BlockSpec((1, D_MODEL, tn), lambda l, n: (l, 0, n)),
                  pl.BlockSpec((1, 1, tn), lambda l, n: (l, 0, n))],
        out_specs=pl.BlockSpec((1, 8, tn), lambda l, n: (l, 0, n)),
        out_shape=jax.ShapeDtypeStruct((l2, 8, 3 * D_MODEL), F32),
        compiler_params=_cparams(("parallel", "parallel")),
    )(c_pad, w_ada, b_ada)


def _rope_kernel(pos_ref, invf_ref, cos_ref, s1_ref, s2_ref):
    ang = pos_ref[0].astype(F32) * invf_ref[...]
    c = jnp.cos(ang)
    s = jnp.sin(ang)
    lane = lax.broadcasted_iota(jnp.int32, ang.shape, 1)
    first = (lane >= NOPE_DIM) & (lane < NOPE_DIM + HALF_ROPE)
    second = (lane >= NOPE_DIM + HALF_ROPE) & (lane < QK_DIM)
    cos_ref[0] = c
    s1_ref[0] = jnp.where(first, -s, 0.0)
    s2_ref[0] = jnp.where(second, s, 0.0)


def _rope_tables(pos3, invf, tm):
    b, s, _ = pos3.shape
    spec = pl.BlockSpec((1, tm, HEAD_PAD), lambda i, j: (i, j, 0))
    shp = jax.ShapeDtypeStruct((b, s, HEAD_PAD), F32)
    return pl.pallas_call(
        _rope_kernel,
        grid=(b, s // tm),
        in_specs=[pl.BlockSpec((1, tm, 1), lambda i, j: (i, j, 0)),
                  pl.BlockSpec((1, HEAD_PAD), lambda i, j: (0, 0))],
        out_specs=[spec, spec, spec],
        out_shape=[shp, shp, shp],
        compiler_params=_cparams(("parallel", "parallel")),
    )(pos3, invf)


def _premix_kernel(x_ref, cos_ref, s1_ref, s2_ref, shift_ref, scale_ref, ng_ref, wz_ref, lg_ref,
                   wq_ref, wk_ref, wv_ref, gm_ref, qg_ref, kg_ref, krg_ref,
                   q_out, k_out, v_out, xr_out, gr_out):
    x = x_ref[0]
    h = _rms_rows(x) * ng_ref[...]
    h = h * (1.0 + scale_ref[0]) + shift_ref[0]
    z = jnp.dot(h.astype(BF16), wz_ref[...], preferred_element_type=F32)
    xr_out[0] = z[:, Q_RANK + KV_RANK + HEAD_PAD:Q_RANK + KV_RANK + HEAD_PAD + LRU_WIDTH]
    gr_out[0] = z[:, Q_RANK + KV_RANK + HEAD_PAD + LRU_WIDTH:]

    lg = lg_ref[...]
    cq = _rms_rows(z[:, :Q_RANK]) * lg[:, :Q_RANK]
    ckv = (_rms_rows(z[:, Q_RANK:Q_RANK + KV_RANK]) * lg[:, Q_RANK:]).astype(BF16)
    kr = z[:, Q_RANK + KV_RANK:Q_RANK + KV_RANK + HEAD_PAD]

    qf = jnp.dot(cq.astype(BF16), wq_ref[...], preferred_element_type=F32)
    kf = jnp.dot(ckv, wk_ref[...], preferred_element_type=F32)
    vf = jnp.dot(ckv, wv_ref[...], preferred_element_type=F32)
    vlane = lax.broadcasted_iota(jnp.int32, vf.shape, 1)
    v_out[0] = jnp.where(vlane % HEAD_PAD == V_DIM, 1.0, vf).astype(BF16)

    cos = cos_ref[0]
    s1 = s1_ref[0]
    s2 = s2_ref[0]
    gm = gm_ref[...]

    def group_norm(blk):
        msq = jnp.dot((blk * blk).astype(BF16), gm, preferred_element_type=F32)
        return blk * lax.rsqrt(msq + EPS)

    def rope(blk):
        return (blk * cos + pltpu.roll(blk, HEAD_PAD - HALF_ROPE, 1) * s1
                + pltpu.roll(blk, HALF_ROPE, 1) * s2)

    kr_rot = rope(group_norm(kr) * krg_ref[...])
    qg = qg_ref[...]
    kg = kg_ref[...]
    for hd in range(MLA_HEADS):
        sl = slice(hd * HEAD_PAD, (hd + 1) * HEAD_PAD)
        qh = rope(group_norm(qf[:, sl]) * qg) * SM_SCALE
        q_out[0, :, sl] = qh.astype(BF16)
        kh = group_norm(kf[:, sl]) * kg + kr_rot
        k_out[0, :, sl] = kh.astype(BF16)


def _premix(x, cos, s1, s2, shift, scale, ng, wz, lgain, wq, wk, wv, gm, qg, kg, krg, tm):
    b, s, d = x.shape
    tok = lambda w: pl.BlockSpec((1, tm, w), lambda i, j: (i, j, 0))
    per_b = pl.BlockSpec((1, 1, d), lambda i, j: (i, 0, 0))
    full = lambda a: pl.BlockSpec(a.shape, lambda i, j: (0,) * a.ndim)
    return pl.pallas_call(
        _premix_kernel,
        grid=(b, s // tm),
        in_specs=[tok(d), tok(HEAD_PAD), tok(HEAD_PAD), tok(HEAD_PAD), per_b, per_b, full(ng), full(wz),
                  full(lgain), full(wq), full(wk), full(wv), full(gm), full(qg), full(kg), full(krg)],
        out_specs=[tok(QK_PAD), tok(QK_PAD), tok(QK_PAD), tok(LRU_WIDTH), tok(LRU_WIDTH)],
        out_shape=[jax.ShapeDtypeStruct((b, s, QK_PAD), BF16),
                   jax.ShapeDtypeStruct((b, s, QK_PAD), BF16),
                   jax.ShapeDtypeStruct((b, s, QK_PAD), BF16),
                   jax.ShapeDtypeStruct((b, s, LRU_WIDTH), F32),
                   jax.ShapeDtypeStruct((b, s, LRU_WIDTH), F32)],
        compiler_params=_cparams(("parallel", "parallel")),
    )(x, cos, s1, s2, shift, scale, ng, wz, lgain, wq, wk, wv, gm, qg, kg, krg)


SUBLANES = 8
ATTN_ROWS = 32
ATTN_HEADS = 4
LANES = 128


def _attn_kernel(q_ref, k_ref, v_ref, pq_ref, pk_ref, o_ref, s_scr, p_scr, m_scr, acc_scr, pq_scr, *, tq, tk):
    assert tq == tk
    qi = pl.program_id(2)
    m_scr[...] = jnp.full(m_scr.shape, NEG, F32)
    acc_scr[...] = jnp.zeros(acc_scr.shape, F32)
    pq_scr[...] = jnp.broadcast_to(pq_ref[0], pq_scr.shape)
    head_lanes = [slice(hh * HEAD_PAD, (hh + 1) * HEAD_PAD) for hh in range(ATTN_HEADS)]

    def chunk(j, width, masked):
        ks = pl.multiple_of(j * tk, tk)
        for hh, sl in enumerate(head_lanes):
            s_scr[hh, :, 0:width] = lax.dot_general(q_ref[0, :, sl], k_ref[0, pl.ds(ks, width), sl],
                                                    (((1,), (1,)), ((), ())), preferred_element_type=F32)
        for hh, sl in enumerate(head_lanes):
            for rb in range(tq // ATTN_ROWS):
                rows = slice(rb * ATTN_ROWS, (rb + 1) * ATTN_ROWS)

                first_row, last_row = rb * ATTN_ROWS, (rb + 1) * ATTN_ROWS - 1
                visible = [not masked or (cb + 1) * LANES - 1 <= first_row for cb in range(width // LANES)]
                hidden = [masked and cb * LANES > last_row for cb in range(width // LANES)]

                def block(cb):
                    blk = s_scr[hh, rows, cb * LANES:(cb + 1) * LANES]
                    if not visible[cb]:
                        keep = pk_ref[0, :, pl.ds(ks + cb * LANES, LANES)] <= pq_scr[rows, :]
                        blk = jnp.where(keep, blk, NEG)
                    return blk

                live = [cb for cb in range(width // LANES) if not hidden[cb]]
                rmax = block(live[0])
                for cb in live[1:]:
                    rmax = jnp.maximum(rmax, block(cb))
                m_old = m_scr[hh, rows, :]
                m_new = jnp.maximum(m_old, jnp.max(rmax, axis=-1, keepdims=True))
                m_scr[hh, rows, :] = m_new
                acc_scr[hh, rows, :] = acc_scr[hh, rows, :] * jnp.exp2(m_old - m_new)
                for cb in range(width // LANES):
                    if hidden[cb]:
                        p_blk = jnp.zeros((ATTN_ROWS, LANES), BF16)
                    else:
                        p_blk = jnp.exp2(block(cb) - m_new).astype(BF16)
                    p_scr[hh, rows, cb * LANES:(cb + 1) * LANES] = p_blk
            acc_scr[hh] += jnp.dot(p_scr[hh, :, 0:width], v_ref[0, pl.ds(ks, width), sl],
                                   preferred_element_type=F32)

    def two_chunks(i, carry):
        chunk(2 * i, 2 * tk, False)
        return carry

    lax.fori_loop(0, qi // 2, two_chunks, 0)

    @pl.when(qi % 2 == 1)
    def _():
        chunk(qi - 1, tk, False)

    chunk(qi, tk, True)

    lane = lax.broadcasted_iota(jnp.int32, (tq, LANES), 1)
    for pair in range(ATTN_HEADS // 2):
        outs = []
        for hh in (2 * pair, 2 * pair + 1):
            acc = acc_scr[hh]
            den = jnp.sum(jnp.where(lane == V_DIM, acc, 0.0), axis=-1, keepdims=True)
            outs.append(acc / den)
        o_ref[0, :, pair * LANES:(pair + 1) * LANES] = jnp.where(
            lane < V_DIM, outs[0], pltpu.roll(outs[1], V_DIM, 1)).astype(o_ref.dtype)


def _attention(q, k, v, pos_col, pos_row, tq, tk):
    b, s, _ = q.shape
    groups = MLA_HEADS // ATTN_HEADS
    width = ATTN_HEADS * HEAD_PAD
    resident = lambda: pl.BlockSpec((1, s, width), lambda i, g, j: (i, 0, g), pipeline_mode=pl.Buffered(1))
    return pl.pallas_call(
        functools.partial(_attn_kernel, tq=tq, tk=tk),
        grid=(b, groups, s // tq),
        in_specs=[pl.BlockSpec((1, tq, width), lambda i, g, j: (i, j, g)),
                  resident(), resident(),
                  pl.BlockSpec((1, tq, 1), lambda i, g, j: (i, j, 0)),
                  pl.BlockSpec((1, 1, s), lambda i, g, j: (i, 0, 0))],
        out_specs=pl.BlockSpec((1, tq, ATTN_HEADS * V_DIM), lambda i, g, j: (i, j, g)),
        out_shape=jax.ShapeDtypeStruct((b, s, MLA_WIDTH), BF16),
        scratch_shapes=[pltpu.VMEM((ATTN_HEADS, tq, 2 * tk), F32), pltpu.VMEM((ATTN_HEADS, tq, 2 * tk), BF16),
                        pltpu.VMEM((ATTN_HEADS, tq, LANES), F32), pltpu.VMEM((ATTN_HEADS, tq, LANES), F32),
                        pltpu.VMEM((tq, LANES), jnp.int32)],
        compiler_params=_cparams(("parallel", "parallel", "arbitrary")),
    )(q, k, v, pos_col, pos_row)


def _gelu_tanh(x):
    return 0.5 * x * (1.0 + jnp.tanh(np.sqrt(2.0 / np.pi).astype(np.float32) * (x + 0.044715 * (x * x * x))))


def _lru_kernel(xr_ref, gr_ref, cw_ref, cb_ref, wg_ref, ba_ref, bx_ref, lam_ref, o_ref, xbuf, hc, *, tm):
    ti = pl.program_id(1)
    halo = SUBLANES

    @pl.when(ti == 0)
    def _():
        xbuf[0:halo, :] = jnp.zeros((halo, LRU_WIDTH), F32)
        hc[...] = jnp.zeros_like(hc)

    @pl.when(ti > 0)
    def _():
        xbuf[0:halo, :] = xbuf[tm:tm + halo, :]

    xbuf[halo:tm + halo, :] = xr_ref[0]
    cw = cw_ref[...]
    xc = cb_ref[...] + cw[CONV_WIDTH - 1:CONV_WIDTH, :] * xbuf[halo:tm + halo, :]
    for tap in range(1, CONV_WIDTH):
        xc = xc + cw[CONV_WIDTH - 1 - tap:CONV_WIDTH - tap, :] * xbuf[halo - tap:tm + halo - tap, :]

    gz = jnp.dot(xc.astype(BF16), wg_ref[...], preferred_element_type=F32)
    r = jax.nn.sigmoid(gz[:, :LRU_WIDTH] + ba_ref[...])
    gi = jax.nn.sigmoid(gz[:, LRU_WIDTH:] + bx_ref[...])
    lam = lam_ref[...]
    log_sig = -(jnp.maximum(-lam, 0.0) + jnp.log1p(jnp.exp(-jnp.abs(lam))))
    log_a = LRU_C * r * log_sig
    a = jnp.exp(log_a)
    u = jnp.sqrt(-jnp.tanh(log_a) * (a * a + 1.0)) * (gi * xc)

    sub = lax.broadcasted_iota(jnp.int32, (tm, LRU_WIDTH), 0) % SUBLANES
    d = 1
    while d < SUBLANES:
        keep = sub >= d
        a_prev = jnp.where(keep, pltpu.roll(a, d, 0), 1.0)
        u_prev = jnp.where(keep, pltpu.roll(u, d, 0), 0.0)
        u = a * u_prev + u
        a = a * a_prev
        d *= 2
    h_in = hc[...]
    groups = []
    for g in range(tm // SUBLANES):
        rows = slice(g * SUBLANES, (g + 1) * SUBLANES)
        blk = a[rows, :] * h_in + u[rows, :]
        groups.append(blk)
        h_in = blk[SUBLANES - 1:SUBLANES, :]
    hc[...] = h_in
    hseq = jnp.concatenate(groups, axis=0)
    o_ref[0] = (hseq * _gelu_tanh(gr_ref[0])).astype(o_ref.dtype)


def _lru(xr, gr, cw, cb, wg, ba, bx, lam, tm):
    b, s, w = xr.shape
    tok = pl.BlockSpec((1, tm, w), lambda i, j: (i, j, 0))
    full = lambda a: pl.BlockSpec(a.shape, lambda i, j: (0,) * a.ndim)
    return pl.pallas_call(
        functools.partial(_lru_kernel, tm=tm),
        grid=(b, s // tm),
        in_specs=[tok, tok, full(cw), full(cb), full(wg), full(ba), full(bx), full(lam)],
        out_specs=tok,
        out_shape=jax.ShapeDtypeStruct((b, s, w), BF16),
        scratch_shapes=[pltpu.VMEM((tm + SUBLANES, w), F32), pltpu.VMEM((1, w), F32)],
        compiler_params=_cparams(("parallel", "arbitrary")),
    )(xr, gr, cw, cb, wg, ba, bx, lam)


ROUTE_I1, ROUTE_I2, ROUTE_W1, ROUTE_W2, ROUTE_R1, ROUTE_R2 = range(6)


def _postmix_kernel(*refs, router):
    if router:
        (a_ref, r_ref, x_ref, gate_ref, og_ref, wo_ref, ng_ref, shift_ref, scale_ref, wr_ref, tri_ref,
         x_out, h_out, route_out, count_out, count_scr) = refs
    else:
        (a_ref, r_ref, x_ref, gate_ref, og_ref, wo_ref, ng_ref, shift_ref, scale_ref,
         x_out, h_out) = refs
    og = og_ref[...]
    ya = (_rms_rows(a_ref[0].astype(F32)) * og[:, :MLA_WIDTH]).astype(BF16)
    yr = (_rms_rows(r_ref[0].astype(F32)) * og[:, MLA_WIDTH:]).astype(BF16)
    y = (jnp.dot(ya, wo_ref[0:MLA_WIDTH, :], preferred_element_type=F32)
         + jnp.dot(yr, wo_ref[MLA_WIDTH:, :], preferred_element_type=F32))
    xn = x_ref[0] + gate_ref[0] * y
    x_out[0] = xn
    h2 = _rms_rows(xn) * ng_ref[...]
    h2 = h2 * (1.0 + scale_ref[0]) + shift_ref[0]
    h_out[0] = h2.astype(h_out.dtype)
    if router:
        @pl.when((pl.program_id(0) == 0) & (pl.program_id(1) == 0))
        def _():
            count_scr[...] = jnp.zeros_like(count_scr)

        h_hi = h2.astype(BF16)
        h_lo = (h2 - h_hi.astype(F32)).astype(BF16)
        wr = wr_ref[...]
        w_hi = wr.astype(BF16)
        w_lo = (wr - w_hi.astype(F32)).astype(BF16)
        logits = (jnp.dot(h_hi, w_hi, preferred_element_type=F32)
                  + (jnp.dot(h_hi, w_lo, preferred_element_type=F32)
                     + jnp.dot(h_lo, w_hi, preferred_element_type=F32)))
        lane = lax.broadcasted_iota(jnp.int32, logits.shape, 1)
        lg = jnp.where(lane < N_EXPERTS, logits, -jnp.inf)
        m1 = jnp.max(lg, axis=-1, keepdims=True)
        i1 = jnp.min(jnp.where(lg == m1, lane, ROUTER_PAD), axis=-1, keepdims=True)
        lg2 = jnp.where(lane == i1, -jnp.inf, lg)
        m2 = jnp.max(lg2, axis=-1, keepdims=True)
        i2 = jnp.min(jnp.where(lg2 == m2, lane, ROUTER_PAD), axis=-1, keepdims=True)
        e2 = jnp.exp(m2 - m1)
        den = 1.0 + e2
        sel1 = lane == i1
        sel2 = lane == i2
        sel = jnp.where(sel1, 1.0, jnp.where(sel2, 1.0, 0.0))
        before = count_scr[...] + jnp.dot(tri_ref[...], sel.astype(BF16), preferred_element_type=F32)
        r1 = jnp.sum(jnp.where(sel1, before, 0.0), axis=-1, keepdims=True)
        r2 = jnp.sum(jnp.where(sel2, before, 0.0), axis=-1, keepdims=True)
        count_scr[...] += jnp.sum(sel, axis=0, keepdims=True)
        count_out[...] = count_scr[...]
        rec = jnp.zeros(logits.shape, F32)
        for ln, val in ((ROUTE_I1, i1.astype(F32)), (ROUTE_I2, i2.astype(F32)), (ROUTE_W1, 1.0 / den),
                        (ROUTE_W2, e2 / den), (ROUTE_R1, r1), (ROUTE_R2, r2)):
            rec = jnp.where(lane == ln, val, rec)
        route_out[0] = rec


def _postmix(attn, rec, x, gate, og, wo, ng, shift, scale, wr, tm):
    b, s, d = x.shape
    router = wr is not None
    tok = lambda w: pl.BlockSpec((1, tm, w), lambda i, j: (i, j, 0))
    per_b = pl.BlockSpec((1, 1, d), lambda i, j: (i, 0, 0))
    full = lambda a: pl.BlockSpec(a.shape, lambda i, j: (0,) * a.ndim)
    in_specs = [tok(MLA_WIDTH), tok(LRU_WIDTH), tok(d), per_b, full(og), full(wo), full(ng), per_b, per_b]
    args = [attn, rec, x, gate, og, wo, ng, shift, scale]
    out_specs = [tok(d), tok(d)]
    out_shape = [jax.ShapeDtypeStruct((b, s, d), F32), jax.ShapeDtypeStruct((b, s, d), F32 if router else BF16)]
    scratch = []
    if router:
        tri = jnp.asarray(np.tril(np.ones((tm, tm), np.float32), -1), dtype=BF16)
        in_specs += [full(wr), full(tri)]
        args += [wr, tri]
        out_specs += [tok(ROUTER_PAD), pl.BlockSpec((1, ROUTER_PAD), lambda i, j: (0, 0))]
        out_shape += [jax.ShapeDtypeStruct((b, s, ROUTER_PAD), F32), jax.ShapeDtypeStruct((1, ROUTER_PAD), F32)]
        scratch = [pltpu.VMEM((1, ROUTER_PAD), F32)]
    return pl.pallas_call(
        functools.partial(_postmix_kernel, router=router),
        grid=(b, s // tm),
        in_specs=in_specs, out_specs=out_specs, out_shape=out_shape, scratch_shapes=scratch,
        compiler_params=_cparams(("arbitrary", "arbitrary") if router else ("parallel", "parallel")),
    )(*args)


FF_SUB = 256


def _swiglu_chunk(h, wg_ref, wu_ref, wd_ref):
    out = None
    for c in range(wg_ref.shape[1] // FF_SUB):
        cols = slice(c * FF_SUB, (c + 1) * FF_SUB)
        g = jnp.dot(h, wg_ref[:, cols].astype(BF16), preferred_element_type=F32)
        u = jnp.dot(h, wu_ref[:, cols].astype(BF16), preferred_element_type=F32)
        act = (g * jax.nn.sigmoid(g)) * u
        part = jnp.dot(act.astype(BF16), wd_ref[cols, :].astype(BF16), preferred_element_type=F32)
        out = part if out is None else out + part
    return out


def _ffn_kernel(h_ref, wg_ref, wu_ref, wd_ref, x_ref, gate_ref, o_ref):
    o_ref[...] = x_ref[...] + gate_ref[0] * _swiglu_chunk(h_ref[...], wg_ref, wu_ref, wd_ref)


def _ffn(h, wg, wu, wd, x, gate, seq, tm):
    n, d = x.shape
    tiles_per_seq = seq // tm
    resident = lambda a: pl.BlockSpec(a.shape, lambda i: (0, 0), pipeline_mode=pl.Buffered(1))
    return pl.pallas_call(
        _ffn_kernel,
        grid=(n // tm,),
        in_specs=[pl.BlockSpec((tm, d), lambda i: (i, 0)),
                  resident(wg), resident(wu), resident(wd),
                  pl.BlockSpec((tm, d), lambda i: (i, 0)),
                  pl.BlockSpec((1, 1, d), lambda i: (i // tiles_per_seq, 0, 0))],
        out_specs=pl.BlockSpec((tm, d), lambda i: (i, 0)),
        out_shape=jax.ShapeDtypeStruct((n, d), F32),
        compiler_params=_cparams(("parallel",)),
    )(h, wg, wu, wd, x, gate)


ROW_UNROLL = 8


def _start_row_copies(tm, copy_of):
    def body(i, c):
        for u in range(ROW_UNROLL):
            for k in range(2):
                copy_of(i * ROW_UNROLL + u, k).start(priority=k)
        return c

    lax.fori_loop(0, tm // ROW_UNROLL, body, 0)


def _wait_bytes_of(ref, sem, times):
    for _ in range(times):
        pltpu.make_async_copy(ref, ref, sem).wait()


def _dispatch_kernel(pad_ref, slot_ref, h_ref, xs_ref, zbuf, sem, *, tm, tm_e):
    @pl.when(pl.program_id(0) == 0)
    def _():
        zbuf[...] = jnp.zeros_like(zbuf)
        for e in range(2 * N_EXPERTS):
            pad = pl.multiple_of(pad_ref[e], 8)
            zero_copy = pltpu.make_async_copy(zbuf, xs_ref.at[pl.ds(pad, tm_e), :], sem)
            zero_copy.start()
            zero_copy.wait()

    _start_row_copies(tm, lambda r, k: pltpu.make_async_copy(
        h_ref.at[pl.ds(r, 1), :], xs_ref.at[pl.ds(slot_ref[0, 0, 2 * r + k], 1), :], sem))
    _wait_bytes_of(h_ref, sem, 2)


def _dispatch(pad_start, slots, h, n_rows, tm, tm_e):
    n, d = h.shape
    grid_spec = pltpu.PrefetchScalarGridSpec(
        num_scalar_prefetch=1,
        grid=(n // tm,),
        in_specs=[pl.BlockSpec((1, 1, 2 * tm), lambda i, pad: (i, 0, 0), memory_space=pltpu.SMEM),
                  pl.BlockSpec((tm, d), lambda i, pad: (i, 0))],
        out_specs=pl.BlockSpec(memory_space=pl.ANY),
        scratch_shapes=[pltpu.VMEM((tm_e, d), F32), pltpu.SemaphoreType.DMA(())])
    return pl.pallas_call(
        functools.partial(_dispatch_kernel, tm=tm, tm_e=tm_e),
        grid_spec=grid_spec,
        out_shape=jax.ShapeDtypeStruct((n_rows, d), F32),
        compiler_params=_cparams(("arbitrary",)),
    )(pad_start, slots, h)


def _expert_kernel(te_ref, nu_ref, x_ref, wg_ref, wu_ref, wd_ref, o_ref, acc, *, n_f):
    del te_ref
    i = pl.program_id(0)
    f = pl.program_id(1)

    @pl.when(f == 0)
    def _():
        acc[...] = jnp.zeros_like(acc)

    @pl.when(i < nu_ref[0])
    def _():
        acc[...] += _swiglu_chunk(x_ref[...].astype(BF16), wg_ref.at[0, 0], wu_ref.at[0, 0], wd_ref.at[0, 0])

    @pl.when(f == n_f - 1)
    def _():
        o_ref[...] = acc[...]


def _experts(tile_expert, n_used, xs, wg, wu, wd, layer, tm, tf):
    p, d = xs.shape
    n_f = wg.shape[3] // tf
    f_of = lambda i, f, nu: jnp.where(i < nu[0], f, n_f - 1)
    grid_spec = pltpu.PrefetchScalarGridSpec(
        num_scalar_prefetch=2,
        grid=(p // tm, n_f),
        in_specs=[pl.BlockSpec((tm, d), lambda i, f, te, nu: (jnp.minimum(i, nu[0] - 1), 0)),
                  pl.BlockSpec((1, 1, d, tf), lambda i, f, te, nu: (layer, te[i], 0, f_of(i, f, nu))),
                  pl.BlockSpec((1, 1, d, tf), lambda i, f, te, nu: (layer, te[i], 0, f_of(i, f, nu))),
                  pl.BlockSpec((1, 1, tf, d), lambda i, f, te, nu: (layer, te[i], f_of(i, f, nu), 0))],
        out_specs=pl.BlockSpec((tm, d), lambda i, f, te, nu: (i, 0)),
        scratch_shapes=[pltpu.VMEM((tm, d), F32)])
    return pl.pallas_call(
        functools.partial(_expert_kernel, n_f=n_f),
        grid_spec=grid_spec,
        out_shape=jax.ShapeDtypeStruct((p, d), F32),
        compiler_params=_cparams(("arbitrary", "arbitrary")),
    )(tile_expert, n_used, xs, wg, wu, wd)


def _combine_kernel(slot_ref, ys_ref, route_ref, x_ref, gate_ref, o_ref, buf, sem, *, tm):
    _start_row_copies(tm, lambda r, k: pltpu.make_async_copy(
        ys_ref.at[pl.ds(slot_ref[0, 0, 2 * r + k], 1), :], buf.at[k, pl.ds(r, 1), :], sem))
    _wait_bytes_of(buf.at[0], sem, 2)
    route = route_ref[...]
    w1 = route[:, ROUTE_W1:ROUTE_W1 + 1]
    w2 = route[:, ROUTE_W2:ROUTE_W2 + 1]
    o_ref[...] = x_ref[...] + gate_ref[0] * (w1 * buf[0] + w2 * buf[1])


def _combine(slots, ys, route, x, gate, seq, tm):
    n, d = x.shape
    tiles_per_seq = seq // tm
    return pl.pallas_call(
        functools.partial(_combine_kernel, tm=tm),
        grid=(n // tm,),
        in_specs=[pl.BlockSpec((1, 1, 2 * tm), lambda i: (i, 0, 0), memory_space=pltpu.SMEM),
                  pl.BlockSpec(memory_space=pl.ANY),
                  pl.BlockSpec((tm, ROUTER_PAD), lambda i: (i, 0)),
                  pl.BlockSpec((tm, d), lambda i: (i, 0)),
                  pl.BlockSpec((1, 1, d), lambda i: (i // tiles_per_seq, 0, 0))],
        out_specs=pl.BlockSpec((tm, d), lambda i: (i, 0)),
        out_shape=jax.ShapeDtypeStruct((n, d), F32),
        scratch_shapes=[pltpu.VMEM((2, tm, d), F32), pltpu.SemaphoreType.DMA(())],
        compiler_params=_cparams(("arbitrary",)),
    )(slots, ys, route, x, gate)


def _moe(h, route, counts, wg, wu, wd, layer, x, gate, seq, tm, tm_e, tf):
    n, d = x.shape
    cnt = counts[0, :N_EXPERTS].astype(jnp.int32)
    tiles = (cnt + tm_e - 1) // tm_e
    tile_end = jnp.cumsum(tiles)
    row_start = (tile_end - tiles) * tm_e
    n_tiles = 2 * n // tm_e + N_EXPERTS
    tile_expert = jnp.minimum(jnp.sum(jnp.arange(n_tiles)[:, None] >= tile_end[None, :], axis=1),
                              N_EXPERTS - 1).astype(jnp.int32)
    n_used = tile_end[-1:].astype(jnp.int32)
    expert = route[:, ROUTE_I1:ROUTE_I2 + 1].astype(jnp.int32)
    rank = route[:, ROUTE_R1:ROUTE_R2 + 1].astype(jnp.int32)
    slots = (row_start[expert] + rank).reshape(n // tm, 1, 2 * tm)
    n_rows = n_tiles * tm_e
    pad_start = jnp.minimum((row_start + cnt) // 8 * 8, n_rows - tm_e)
    tail_start = jnp.minimum(tile_end[-1] + jnp.arange(N_EXPERTS), n_tiles - 1) * tm_e
    pad_start = jnp.concatenate([pad_start, tail_start]).astype(jnp.int32)
    xs = _dispatch(pad_start, slots, h, n_rows, tm, tm_e)
    ys = _experts(tile_expert, n_used, xs, wg, wu, wd, layer, tm_e, tf)
    return _combine(slots, ys, route, x, gate, seq, tm)


def _group_mean_matrix():
    g = np.zeros((HEAD_PAD, HEAD_PAD), np.float32)
    g[:NOPE_DIM, :NOPE_DIM] = 1.0 / NOPE_DIM
    g[NOPE_DIM:QK_DIM, NOPE_DIM:QK_DIM] = 1.0 / ROPE_DIM
    return jnp.asarray(g, dtype=BF16)


def _pad_cols(a, left, total):
    return jnp.pad(a, ((0, 0), (left, total - left - a.shape[1])))


def _layer_weights(l, w_in, w_q_up, w_kv_up, q_gain, k_gain, w_a, w_x):
    wi = w_in[l]
    n_lat = Q_RANK + KV_RANK
    wz = jnp.concatenate([wi[:, :n_lat], _pad_cols(wi[:, n_lat:n_lat + ROPE_DIM], NOPE_DIM, HEAD_PAD),
                          wi[:, n_lat + ROPE_DIM:]], axis=1).astype(BF16)
    wq = jnp.pad(w_q_up[l].reshape(Q_RANK, MLA_HEADS, QK_DIM),
                 ((0, 0), (0, 0), (0, HEAD_PAD - QK_DIM))).reshape(Q_RANK, QK_PAD).astype(BF16)
    wkv = w_kv_up[l].reshape(KV_RANK, MLA_HEADS, NOPE_DIM + V_DIM)
    wk = jnp.pad(wkv[..., :NOPE_DIM], ((0, 0), (0, 0), (0, HEAD_PAD - NOPE_DIM)))
    wk = wk.reshape(KV_RANK, QK_PAD).astype(BF16)
    wv = jnp.pad(wkv[..., NOPE_DIM:], ((0, 0), (0, 0), (0, HEAD_PAD - V_DIM)))
    wv = wv.reshape(KV_RANK, QK_PAD).astype(BF16)
    qg = _pad_cols(q_gain[l][None, :], 0, HEAD_PAD)
    kg = _pad_cols(k_gain[l][None, :NOPE_DIM], 0, HEAD_PAD)
    krg = _pad_cols(k_gain[l][None, NOPE_DIM:], NOPE_DIM, HEAD_PAD)
    eye = jnp.eye(LRU_BLOCKS, dtype=F32)
    blockdiag = lambda w: jnp.einsum('gij,gh->gihj', w, eye).reshape(LRU_WIDTH, LRU_WIDTH)
    wgate = jnp.concatenate([blockdiag(w_a[l]), blockdiag(w_x[l])], axis=1).astype(BF16)
    return wz, wq, wk, wv, qg, kg, krg, wgate


def kernel(x, c, positions, w_ada, b_ada, norm_gain, w_in, latent_gain, w_q_up, w_kv_up, q_gain, k_gain,
           conv_w, conv_b, w_a, b_a, w_x, b_x, lru_lambda, out_gain, w_out, ffn_w_gate, ffn_w_up,
           ffn_w_down, router_w, moe_w_gate, moe_w_up, moe_w_down):
    b, s, d = x.shape
    n = b * s
    tm = min(1024, s)
    tq = min(512, s)
    tk = min(512, s)

    c_pad = jnp.pad(c, ((0, 8 - b), (0, 0)))
    mods = _ada_all(c_pad, w_ada.reshape(DEPTH * 2, d, 3 * d), b_ada.reshape(DEPTH * 2, 1, 3 * d))
    mods = mods[:, :b].reshape(DEPTH, 2, b, 1, 3, d)
    shift_of = lambda l, i: mods[l, i, :, :, 0]
    scale_of = lambda l, i: mods[l, i, :, :, 1]
    gate_of = lambda l, i: mods[l, i, :, :, 2]

    inv_freq = ROPE_THETA ** (-jnp.arange(0, ROPE_DIM, 2, dtype=F32) / ROPE_DIM)
    invf = _pad_cols(jnp.concatenate([inv_freq, inv_freq])[None, :], NOPE_DIM, HEAD_PAD)
    pos_col = positions.reshape(b, s, 1)
    pos_row = positions.reshape(b, 1, s)
    cos, s1, s2 = _rope_tables(pos_col, invf, tm)
    gm = _group_mean_matrix()

    for l in range(DEPTH):
        wz, wq, wk, wv, qg, kg, krg, wgate = _layer_weights(l, w_in, w_q_up, w_kv_up, q_gain, k_gain, w_a, w_x)
        q, k, v, xr, gr = _premix(x, cos, s1, s2, shift_of(l, 0), scale_of(l, 0), norm_gain[l, 0][None, :],
                                  wz, latent_gain[l][None, :], wq, wk, wv, gm, qg, kg, krg, tm)
        attn = _attention(q, k, v, pos_col, pos_row, tq, tk)
        rec = _lru(xr, gr, conv_w[l], conv_b[l][None, :], wgate, b_a[l][None, :], b_x[l][None, :],
                   lru_lambda[l][None, :], tm)
        j = l // 2
        moe = l % 2 == 1
        wr = jnp.pad(router_w[j], ((0, 0), (0, ROUTER_PAD - N_EXPERTS))) if moe else None
        outs = _postmix(attn, rec, x, gate_of(l, 0), out_gain[l][None, :], w_out[l].astype(BF16),
                        norm_gain[l, 1][None, :], shift_of(l, 1), scale_of(l, 1), wr, tm)
        x_mid, h2 = outs[0].reshape(n, d), outs[1].reshape(n, d)
        if moe:
            x = _moe(h2, outs[2].reshape(n, ROUTER_PAD), outs[3], moe_w_gate, moe_w_up, moe_w_down, j,
                     x_mid, gate_of(l, 1), s, tm, min(1024, n), 512)
        else:
            x = _ffn(h2, ffn_w_gate[j].astype(BF16), ffn_w_up[j].astype(BF16), ffn_w_down[j].astype(BF16),
                     x_mid, gate_of(l, 1), s, tm)
        x = x.reshape(b, s, d)
    return x
```
